```python
import math
import jax, jax.numpy as jnp
from jax import lax
import numpy as np

D_MODEL = 1024
BATCH = 4
SEQ = 4096
DEPTH = 1

CONV_WIDTH = D_MODEL // 2
CONV_K = 3
DIFF_HEADS = 4
DIFF_HEAD_DIM = 64
DIFF_V_DIM = 2 * DIFF_HEAD_DIM
DIFF_WIDTH = DIFF_HEADS * DIFF_V_DIM
N_BRANCHES = 2
IN_COLS = 3 * CONV_WIDTH + 3 * DIFF_WIDTH + N_BRANCHES * D_MODEL
D_FF = 4 * D_MODEL
Q_BLOCK = 128
RMS_EPS = 1e-6

kernel_name = "hybrid_shortconv_diffattn_gated_merge"


def rmsnorm(x, g):
    xf = x.astype(jnp.float32)
    y = xf * lax.rsqrt(jnp.mean(xf * xf, axis=-1, keepdims=True) + RMS_EPS)
    return (y * g.astype(jnp.float32)).astype(x.dtype)


def alibi_slopes(n_heads):
    return np.array([2.0 ** (-8.0 * (h + 1) / n_heads) for h in range(n_heads)], dtype=np.float32)


def lambda_init_fn(layer_idx):
    return 0.8 - 0.6 * math.exp(-0.3 * layer_idx)


def short_gated_conv(b_gate, c_gate, v, conv_w):
    h = c_gate * v
    hp = jnp.pad(h, ((0, 0), (CONV_K - 1, 0), (0, 0)))
    S = h.shape[1]
    conv = (conv_w[0] * hp[:, 0:S] + conv_w[1] * hp[:, 1:S + 1] + conv_w[2] * hp[:, 2:S + 2])
    return b_gate * conv


def diff_attention(q, k, v, lam, slopes):
    Bsz, S, H = q.shape[0], q.shape[1], q.shape[2]
    nb = S // Q_BLOCK
    scale = DIFF_HEAD_DIM ** -0.5
    q_blocks = q.reshape(Bsz, nb, Q_BLOCK, H, 2, DIFF_HEAD_DIM).transpose(1, 0, 2, 3, 4, 5)
    starts = jnp.arange(nb, dtype=jnp.int32) * Q_BLOCK
    kpos = jnp.arange(S, dtype=jnp.int32)
    slopes_f = jnp.asarray(slopes, dtype=jnp.float32)

    def one_block(args):
        qb, start = args
        s = jnp.einsum('bqhcd,bkhcd->bhcqk', qb, k).astype(jnp.float32) * scale
        qpos = start + jnp.arange(Q_BLOCK, dtype=jnp.int32)
        dist = (qpos[:, None] - kpos[None, :])
        bias = -slopes_f[:, None, None, None] * dist.astype(jnp.float32)[None, None]
        s = jnp.where(dist >= 0, s + bias, -jnp.inf)
        p = jax.nn.softmax(s, axis=-1)
        a = p[:, :, 0] - lam * p[:, :, 1]
        return jnp.einsum('bhqk,bkhv->bqhv', a.astype(v.dtype), v)

    o = lax.map(one_block, (q_blocks, starts))
    return o.transpose(1, 0, 2, 3, 4).reshape(Bsz, S, H, DIFF_V_DIM)


def setup_inputs(seed: int = 0) -> dict:
    key = jax.random.key(seed)
    ks = jax.random.split(key, 20)
    f32 = jnp.float32
    nrm = lambda k, shape, s: (jax.random.normal(k, shape, f32) * s)
    gain = lambda k, shape: 1.0 + 0.05 * jax.random.normal(k, shape, f32)
    return {
        "x": jax.random.normal(ks[0], (BATCH, SEQ, D_MODEL), f32),
        "norm_mix_g": gain(ks[1], (DEPTH, D_MODEL)),
        "w_in": nrm(ks[2], (DEPTH, D_MODEL, IN_COLS), D_MODEL ** -0.5),
        "b_gate": nrm(ks[3], (DEPTH, N_BRANCHES * D_MODEL), 0.02),
        "conv_w": nrm(ks[4], (DEPTH, CONV_K, CONV_WIDTH), CONV_K ** -0.5),
        "lambda_q1": nrm(ks[5], (DEPTH, DIFF_HEAD_DIM), 0.1),
        "lambda_k1": nrm(ks[6], (DEPTH, DIFF_HEAD_DIM), 0.1),
        "lambda_q2": nrm(ks[7], (DEPTH, DIFF_HEAD_DIM), 0.1),
        "lambda_k2": nrm(ks[8], (DEPTH, DIFF_HEAD_DIM), 0.1),
        "subln_g": gain(ks[9], (DEPTH, DIFF_V_DIM)),
        "w_a_out": nrm(ks[10], (DEPTH, CONV_WIDTH, D_MODEL), CONV_WIDTH ** -0.5),
        "w_b_out": nrm(ks[11], (DEPTH, DIFF_WIDTH, D_MODEL), DIFF_WIDTH ** -0.5),
        "w_o": nrm(ks[12], (DEPTH, D_MODEL, D_MODEL), D_MODEL ** -0.5),
        "norm_mlp_g": gain(ks[13], (DEPTH, D_MODEL)),
        "w_mlp_in": nrm(ks[14], (DEPTH, D_MODEL, D_FF), D_MODEL ** -0.5),
        "w_mlp_out": nrm(ks[15], (DEPTH, D_FF, D_MODEL), D_FF ** -0.5),
        "norm_final_g": gain(ks[16], (D_MODEL,)),
    }


def reference(x, norm_mix_g, w_in, b_gate, conv_w, lambda_q1, lambda_k1, lambda_q2, lambda_k2,
              subln_g, w_a_out, w_b_out, w_o, norm_mlp_g, w_mlp_in, w_mlp_out, norm_final_g):
    Bsz, S, _ = x.shape
    slopes = alibi_slopes(DIFF_HEADS)
    c0 = CONV_WIDTH
    q0 = 3 * CONV_WIDTH
    g0 = q0 + 3 * DIFF_WIDTH
    for l in range(DEPTH):
        xn = rmsnorm(x, norm_mix_g[l])
        u = xn @ w_in[l]
        b_g, c_g, v_a = u[..., 0:c0], u[..., c0:2 * c0], u[..., 2 * c0:3 * c0]
        q = u[..., q0:q0 + DIFF_WIDTH].reshape(Bsz, S, DIFF_HEADS, 2, DIFF_HEAD_DIM)
        k = u[..., q0 + DIFF_WIDTH:q0 + 2 * DIFF_WIDTH].reshape(Bsz, S, DIFF_HEADS, 2, DIFF_HEAD_DIM)
        v = u[..., q0 + 2 * DIFF_WIDTH:g0].reshape(Bsz, S, DIFF_HEADS, DIFF_V_DIM)
        gates = jax.nn.sigmoid((u[..., g0:] + b_gate[l]).astype(jnp.float32)).astype(x.dtype)
        g_a, g_b = gates[..., :D_MODEL], gates[..., D_MODEL:]

        y_a = short_gated_conv(b_g, c_g, v_a, conv_w[l]) @ w_a_out[l]

        lam_init = lambda_init_fn(l)
        lam = (jnp.exp(jnp.sum(lambda_q1[l].astype(jnp.float32) * lambda_k1[l].astype(jnp.float32)))
               - jnp.exp(jnp.sum(lambda_q2[l].astype(jnp.float32) * lambda_k2[l].astype(jnp.float32)))
               + lam_init)
        o = diff_attention(q, k, v, lam, slopes)
        o = rmsnorm(o, subln_g[l]) * (1.0 - lam_init)
        y_b = o.reshape(Bsz, S, DIFF_WIDTH) @ w_b_out[l]

        x = x + (g_a * y_a + g_b * y_b) @ w_o[l]

        h = rmsnorm(x, norm_mlp_g[l]) @ w_mlp_in[l]
        x = x + jnp.square(jax.nn.relu(h)) @ w_mlp_out[l]
    return rmsnorm(x, norm_final_g)
```

```python
import functools
import math

import jax
import jax.numpy as jnp
import numpy as np
from jax import lax
from jax.experimental import pallas as pl
from jax.experimental.pallas import tpu as pltpu

F32 = jnp.float32
BF16 = jnp.bfloat16

RMS_EPS = 1e-6
CONV_K = 3
DIFF_HEADS = 4
DIFF_HEAD_DIM = 64
DIFF_V_DIM = 2 * DIFF_HEAD_DIM

SUBLANES = 8
VMEM_LIMIT_BYTES = 56 * 1024 * 1024

TOKEN_TILE = 512
ATTN_TILE = 256
FF_CHUNK = 1024


def _rms_scale(x):
    return lax.rsqrt(jnp.mean(x * x, axis=-1, keepdims=True) + RMS_EPS)


def _dot(a, b):
    return jnp.dot(a, b, preferred_element_type=F32)


def _dot_nt(a, b):
    return lax.dot_general(a, b, (((1,), (1,)), ((), ())), preferred_element_type=F32)


def _resident(shape):
    return pl.BlockSpec(shape, lambda *_: (0,) * len(shape), pipeline_mode=pl.Buffered(1))


def _in_proj_kernel(x_ref, g_ref, w_in_ref, w_vt_ref, b_gate_ref, conv_w_ref, w_a_ref,
                    lq1_ref, lk1_ref, lq2_ref, lk2_ref,
                    q_ref, k_ref, vt_ref, gaya_ref, gb_ref, lam_ref,
                    h_ref, *, tiles_per_seq, conv_width, diff_width, d_model, lam_init):
    i = pl.program_id(0)
    tm = x_ref.shape[0]
    c0 = conv_width
    q0 = 3 * conv_width
    g0 = q0 + 3 * diff_width

    @pl.when(i == 0)
    def _():
        lam = (jnp.exp(jnp.sum(lq1_ref[...] * lk1_ref[...], axis=-1, keepdims=True))
               - jnp.exp(jnp.sum(lq2_ref[...] * lk2_ref[...], axis=-1, keepdims=True))
               + lam_init)
        lam_ref[...] = jnp.broadcast_to(lam, lam_ref.shape)

    x = x_ref[...]
    xn = ((x * _rms_scale(x)) * g_ref[...]).astype(BF16)

    uc = _dot(xn, w_in_ref[:, 0:q0])
    h = uc[:, c0:2 * c0] * uc[:, 2 * c0:3 * c0]

    @pl.when(i % tiles_per_seq == 0)
    def _():
        h_ref[0:SUBLANES, :] = jnp.zeros((SUBLANES, c0), F32)

    h_ref[SUBLANES:SUBLANES + tm, :] = h
    cw = conv_w_ref[...]
    conv = (cw[0:1, :] * h_ref[SUBLANES - 2:SUBLANES - 2 + tm, :]
            + cw[1:2, :] * h_ref[SUBLANES - 1:SUBLANES - 1 + tm, :]
            + cw[2:3, :] * h)
    h_ref[0:SUBLANES, :] = h[tm - SUBLANES:tm, :]
    y_a = _dot((uc[:, 0:c0] * conv).astype(BF16), w_a_ref[...])

    gates = jax.nn.sigmoid(_dot(xn, w_in_ref[:, g0:g0 + 2 * d_model]) + b_gate_ref[...])
    gaya_ref[...] = (gates[:, 0:d_model] * y_a).astype(BF16)
    gb_ref[...] = gates[:, d_model:2 * d_model].astype(BF16)

    scale = DIFF_HEAD_DIM ** -0.5
    q_ref[...] = (_dot(xn, w_in_ref[:, q0:q0 + diff_width]) * scale).astype(BF16)
    k_ref[...] = _dot(xn, w_in_ref[:, q0 + diff_width:q0 + 2 * diff_width]).astype(BF16)
    vt_ref[...] = _dot_nt(w_vt_ref[...], xn).astype(BF16)


def _in_proj(x2d, g, w_in, w_vt, b_gate, conv_w, w_a, lq1, lk1, lq2, lk2, *, seq, lam_init):
    n, d_model = x2d.shape
    conv_width = conv_w.shape[1]
    diff_width = w_vt.shape[0]
    tm = TOKEN_TILE
    assert seq % tm == 0 and n % seq == 0
    row = lambda i: (i, 0)
    kern = functools.partial(
        _in_proj_kernel, tiles_per_seq=seq // tm, conv_width=conv_width,
        diff_width=diff_width, d_model=d_model, lam_init=lam_init)
    return pl.pallas_call(
        kern,
        name="in_proj",
        grid=(n // tm,),
        in_specs=[
            pl.BlockSpec((tm, d_model), row),
            _resident(g.shape),
            _resident(w_in.shape),
            _resident(w_vt.shape),
            _resident(b_gate.shape),
            _resident(conv_w.shape),
            _resident(w_a.shape),
            _resident(lq1.shape), _resident(lk1.shape), _resident(lq2.shape), _resident(lk2.shape),
        ],
        out_specs=[
            pl.BlockSpec((tm, diff_width), row),
            pl.BlockSpec((tm, diff_width), row),
            pl.BlockSpec((diff_width, tm), lambda i: (0, i)),
            pl.BlockSpec((tm, d_model), row),
            pl.BlockSpec((tm, d_model), row),
            pl.BlockSpec((SUBLANES, 128), lambda i: (0, 0)),
        ],
        out_shape=[
            jax.ShapeDtypeStruct((n, diff_width), BF16),
            jax.ShapeDtypeStruct((n, diff_width), BF16),
            jax.ShapeDtypeStruct((diff_width, n), BF16),
            jax.ShapeDtypeStruct((n, d_model), BF16),
            jax.ShapeDtypeStruct((n, d_model), BF16),
            jax.ShapeDtypeStruct((SUBLANES, 128), F32),
        ],
        scratch_shapes=[pltpu.VMEM((SUBLANES + tm, conv_width), F32)],
        compiler_params=pltpu.CompilerParams(
            dimension_semantics=("arbitrary",), vmem_limit_bytes=VMEM_LIMIT_BYTES),
    )(x2d, g, w_in, w_vt, b_gate, conv_w, w_a, lq1, lk1, lq2, lk2)


def _attn_kernel(slopes_ref, q_ref, k_ref, vt_ref, lam_ref, g_ref, o_ref,
                 q2_ref, rel_ref, m_ref, l_ref, acc_ref, *, out_scale):
    h = pl.program_id(1)
    i = pl.program_id(2)
    t = q_ref.shape[0]
    d = DIFF_HEAD_DIM
    slope = slopes_ref[h]

    @pl.when(i == 0)
    def _():
        key = lax.broadcasted_iota(jnp.int32, (t, 2 * t), 0)
        qry = lax.broadcasted_iota(jnp.int32, (t, 2 * t), 1)
        qry = jnp.where(qry >= t, qry - t, qry)
        rel_ref[...] = (key - qry).astype(F32) * slope

    q = q_ref[...]
    lane = lax.broadcasted_iota(jnp.int32, q.shape, 1)
    zero = jnp.zeros_like(q)
    q2_ref[0:t, :] = jnp.where(lane < d, q, zero)
    q2_ref[t:2 * t, :] = jnp.where(lane >= d, q, zero)

    def scores(j):
        start = pl.multiple_of(j * t, t)
        return _dot_nt(k_ref[pl.ds(start, t), :], q2_ref[...]) + rel_ref[...], start

    s, start = scores(i)
    s = jnp.where(rel_ref[...] <= 0.0, s, -jnp.inf)
    m = jnp.max(s, axis=0, keepdims=True)
    p = jnp.exp(s - m)
    m_ref[...] = m
    l_ref[...] = jnp.sum(p, axis=0, keepdims=True)
    acc_ref[...] = _dot(vt_ref[:, pl.ds(start, t)], p.astype(BF16))

    def body(j, carry):
        s, start = scores(j)
        off = slope * ((j - i) * t).astype(F32)
        m_old = m_ref[...]
        m_new = jnp.maximum(m_old, jnp.max(s, axis=0, keepdims=True) + off)
        alpha = jnp.exp(m_old - m_new)
        p = jnp.exp(s - (m_new - off))
        m_ref[...] = m_new
        l_ref[...] = alpha * l_ref[...] + jnp.sum(p, axis=0, keepdims=True)
        acc_ref[...] = alpha * acc_ref[...] + _dot(vt_ref[:, pl.ds(start, t)], p.astype(BF16))
        return carry

    lax.fori_loop(0, i, body, 0)

    on = acc_ref[...] * (1.0 / l_ref[...])
    lam = lam_ref[0:1, 0:1]
    o = jnp.transpose(on[:, 0:t] - lam * on[:, t:2 * t])
    o_ref[...] = (((o * _rms_scale(o)) * g_ref[...]) * out_scale).astype(o_ref.dtype)


def _diff_attention(q, k, vt, lam, subln_g, *, batch, seq, out_scale):
    n, width = q.shape
    heads = width // DIFF_V_DIM
    t = ATTN_TILE
    assert seq % t == 0
    nq = seq // t
    slopes = jnp.asarray(
        np.array([2.0 ** (-8.0 * (h + 1) / heads) for h in range(heads)], dtype=np.float32))
    grid_spec = pltpu.PrefetchScalarGridSpec(
        num_scalar_prefetch=1,
        grid=(batch, heads, nq),
        in_specs=[
            pl.BlockSpec((t, DIFF_V_DIM), lambda b, h, i, s: (b * nq + i, h)),
            pl.BlockSpec((seq, DIFF_V_DIM), lambda b, h, i, s: (b, h)),
            pl.BlockSpec((DIFF_V_DIM, seq), lambda b, h, i, s: (h, b)),
            pl.BlockSpec(lam.shape, lambda b, h, i, s: (0, 0)),
            pl.BlockSpec(subln_g.shape, lambda b, h, i, s: (0, 0)),
        ],
        out_specs=pl.BlockSpec((t, DIFF_V_DIM), lambda b, h, i, s: (b * nq + i, h)),
        scratch_shapes=[
            pltpu.VMEM((2 * t, DIFF_V_DIM), BF16),
            pltpu.VMEM((t, 2 * t), F32),
            pltpu.VMEM((1, 2 * t), F32),
            pltpu.VMEM((1, 2 * t), F32),
            pltpu.VMEM((DIFF_V_DIM, 2 * t), F32),
        ],
    )
    return pl.pallas_call(
        functools.partial(_attn_kernel, out_scale=out_scale),
        name="diff_attn",
        grid_spec=grid_spec,
        out_shape=jax.ShapeDtypeStruct((n, width), BF16),
        compiler_params=pltpu.CompilerParams(
            dimension_semantics=("arbitrary", "arbitrary", "arbitrary"),
            vmem_limit_bytes=VMEM_LIMIT_BYTES),
    )(slopes, q, k, vt, lam, subln_g)


def _out_mlp_kernel(x_ref, gaya_ref, gb_ref, on_ref, w_b_ref, w_o_ref, g_mlp_ref,
                    w1_ref, w2_ref, g_fin_ref, out_ref, *, final_norm):
    y_b = _dot(on_ref[...], w_b_ref[...])
    merged = gaya_ref[...].astype(F32) + gb_ref[...].astype(F32) * y_b
    x1 = x_ref[...] + _dot(merged.astype(BF16), w_o_ref[...])
    xn = ((x1 * _rms_scale(x1)) * g_mlp_ref[...]).astype(BF16)
    d_ff = w1_ref.shape[1]
    acc = x1
    for c in range(0, d_ff, FF_CHUNK):
        hc = jnp.maximum(_dot(xn, w1_ref[:, c:c + FF_CHUNK]), 0.0)
        acc = acc + _dot((hc * hc).astype(BF16), w2_ref[c:c + FF_CHUNK, :])
    if final_norm:
        acc = (acc * _rms_scale(acc)) * g_fin_ref[...]
    out_ref[...] = acc


def _out_mlp(x2d, gaya, gb, on, w_b, w_o, g_mlp, w1, w2, g_fin, *, final_norm):
    n, d_model = x2d.shape
    tm = TOKEN_TILE
    assert n % tm == 0 and w1.shape[1] % FF_CHUNK == 0
    row = lambda i: (i, 0)
    return pl.pallas_call(
        functools.partial(_out_mlp_kernel, final_norm=final_norm),
        name="out_mlp",
        grid=(n // tm,),
        in_specs=[
            pl.BlockSpec((tm, d_model), row),
            pl.BlockSpec((tm, d_model), row),
            pl.BlockSpec((tm, d_model), row),
            pl.BlockSpec((tm, on.shape[1]), row),
            _resident(w_b.shape),
            _resident(w_o.shape),
            _resident(g_mlp.shape),
            _resident(w1.shape),
            _resident(w2.shape),
            _resident(g_fin.shape),
        ],
        out_specs=pl.BlockSpec((tm, d_model), row),
        out_shape=jax.ShapeDtypeStruct((n, d_model), F32),
        compiler_params=pltpu.CompilerParams(
            dimension_semantics=("arbitrary",), vmem_limit_bytes=VMEM_LIMIT_BYTES),
    )(x2d, gaya, gb, on, w_b, w_o, g_mlp, w1, w2, g_fin)


def _lambda_init(layer_idx):
    return 0.8 - 0.6 * math.exp(-0.3 * layer_idx)


def kernel(x, norm_mix_g, w_in, b_gate, conv_w, lambda_q1, lambda_k1, lambda_q2, lambda_k2,
           subln_g, w_a_out, w_b_out, w_o, norm_mlp_g, w_mlp_in, w_mlp_out, norm_final_g):
    batch, seq, d_model = x.shape
    depth = w_in.shape[0]
    conv_width = conv_w.shape[2]
    diff_width = w_b_out.shape[1]
    v0 = 3 * conv_width + 2 * diff_width
    x2d = x.reshape(batch * seq, d_model)
    row = lambda a: a.reshape(1, -1)
    for l in range(depth):
        lam_init = _lambda_init(l)
        w_in_l = w_in[l].astype(BF16)
        w_vt = jnp.transpose(w_in[l][:, v0:v0 + diff_width]).astype(BF16)
        q, k, vt, gaya, gb, lam = _in_proj(
            x2d, row(norm_mix_g[l]), w_in_l, w_vt, row(b_gate[l]), conv_w[l],
            w_a_out[l].astype(BF16), row(lambda_q1[l]), row(lambda_k1[l]),
            row(lambda_q2[l]), row(lambda_k2[l]), seq=seq, lam_init=lam_init)
        on = _diff_attention(q, k, vt, lam, row(subln_g[l]), batch=batch, seq=seq,
                             out_scale=1.0 - lam_init)
        x2d = _out_mlp(
            x2d, gaya, gb, on, w_b_out[l].astype(BF16), w_o[l].astype(BF16),
            row(norm_mlp_g[l]), w_mlp_in[l].astype(BF16), w_mlp_out[l].astype(BF16),
            row(norm_final_g), final_norm=(l == depth - 1))
    return x2d.reshape(batch, seq, d_model)
```

```python
import functools
import math

import jax
import jax.numpy as jnp
import numpy as np
from jax import lax
from jax.experimental import pallas as pl
from jax.experimental.pallas import tpu as pltpu

F32 = jnp.float32
BF16 = jnp.bfloat16

RMS_EPS = 1e-6
CONV_K = 3
DIFF_HEADS = 4
DIFF_HEAD_DIM = 64
DIFF_V_DIM = 2 * DIFF_HEAD_DIM

LOG2E = math.log2(math.e)
SUM_ROWS = 16
VT_ROWS = DIFF_V_DIM + SUM_ROWS
HEADS_PER_STEP = 4

SUBLANES = 8
VMEM_LIMIT_BYTES = 56 * 1024 * 1024

TOKEN_TILE = 512
ATTN_TILE = 256
FF_CHUNK = 1024


def _rms_scale(x):
    return lax.rsqrt(jnp.mean(x * x, axis=-1, keepdims=True) + RMS_EPS)


def _dot(a, b):
    return jnp.dot(a, b, preferred_element_type=F32)


def _dot_nt(a, b):
    return lax.dot_general(a, b, (((1,), (1,)), ((), ())), preferred_element_type=F32)


def _resident(shape):
    return pl.BlockSpec(shape, lambda *_: (0,) * len(shape), pipeline_mode=pl.Buffered(1))


def _in_proj_kernel(x_ref, g_ref, w_in_ref, w_vt_ref, b_gate_ref, conv_w_ref, w_a_ref,
                    lq1_ref, lk1_ref, lq2_ref, lk2_ref,
                    q_ref, k_ref, vt_ref, gaya_ref, gb_ref, lam_ref,
                    h_ref, *, tiles_per_seq, conv_width, diff_width, d_model, lam_init):
    i = pl.program_id(0)
    tm = x_ref.shape[0]
    c0 = conv_width
    q0 = 3 * conv_width
    g0 = q0 + 3 * diff_width

    @pl.when(i == 0)
    def _():
        lam = (jnp.exp(jnp.sum(lq1_ref[...] * lk1_ref[...], axis=-1, keepdims=True))
               - jnp.exp(jnp.sum(lq2_ref[...] * lk2_ref[...], axis=-1, keepdims=True))
               + lam_init)
        lam_ref[...] = jnp.broadcast_to(lam, lam_ref.shape)

    x = x_ref[...]
    xn = ((x * _rms_scale(x)) * g_ref[...]).astype(BF16)

    uc = _dot(xn, w_in_ref[:, 0:q0])
    h = uc[:, c0:2 * c0] * uc[:, 2 * c0:3 * c0]

    @pl.when(i % tiles_per_seq == 0)
    def _():
        h_ref[0:SUBLANES, :] = jnp.zeros((SUBLANES, c0), F32)

    h_ref[SUBLANES:SUBLANES + tm, :] = h
    cw = conv_w_ref[...]
    conv = (cw[0:1, :] * h_ref[SUBLANES - 2:SUBLANES - 2 + tm, :]
            + cw[1:2, :] * h_ref[SUBLANES - 1:SUBLANES - 1 + tm, :]
            + cw[2:3, :] * h)
    h_ref[0:SUBLANES, :] = h[tm - SUBLANES:tm, :]
    y_a = _dot((uc[:, 0:c0] * conv).astype(BF16), w_a_ref[...])

    gates = jax.nn.sigmoid(_dot(xn, w_in_ref[:, g0:g0 + 2 * d_model]) + b_gate_ref[...])
    gaya_ref[...] = (gates[:, 0:d_model] * y_a).astype(BF16)
    gb_ref[...] = gates[:, d_model:2 * d_model].astype(BF16)

    scale = DIFF_HEAD_DIM ** -0.5 * LOG2E
    q_ref[...] = (_dot(xn, w_in_ref[:, q0:q0 + diff_width]) * scale).astype(BF16)
    k_ref[...] = _dot(xn, w_in_ref[:, q0 + diff_width:q0 + 2 * diff_width]).astype(BF16)
    vt = _dot_nt(w_vt_ref[...], xn).astype(BF16)
    for hh in range(diff_width // DIFF_V_DIM):
        r0 = hh * VT_ROWS
        vt_ref[r0:r0 + DIFF_V_DIM, :] = vt[hh * DIFF_V_DIM:(hh + 1) * DIFF_V_DIM, :]
        vt_ref[r0 + DIFF_V_DIM:r0 + VT_ROWS, :] = jnp.ones((SUM_ROWS, tm), BF16)


def _in_proj(x2d, g, w_in, w_vt, b_gate, conv_w, w_a, lq1, lk1, lq2, lk2, *, seq, lam_init):
    n, d_model = x2d.shape
    conv_width = conv_w.shape[1]
    diff_width = w_vt.shape[0]
    tm = TOKEN_TILE
    assert seq % tm == 0 and n % seq == 0
    vt_rows = (diff_width // DIFF_V_DIM) * VT_ROWS
    row = lambda i: (i, 0)
    kern = functools.partial(
        _in_proj_kernel, tiles_per_seq=seq // tm, conv_width=conv_width,
        diff_width=diff_width, d_model=d_model, lam_init=lam_init)
    return pl.pallas_call(
        kern,
        name="in_proj",
        grid=(n // tm,),
        in_specs=[
            pl.BlockSpec((tm, d_model), row),
            _resident(g.shape),
            _resident(w_in.shape),
            _resident(w_vt.shape),
            _resident(b_gate.shape),
            _resident(conv_w.shape),
            _resident(w_a.shape),
            _resident(lq1.shape), _resident(lk1.shape), _resident(lq2.shape), _resident(lk2.shape),
        ],
        out_specs=[
            pl.BlockSpec((tm, diff_width), row),
            pl.BlockSpec((tm, diff_width), row),
            pl.BlockSpec((vt_rows, tm), lambda i: (0, i)),
            pl.BlockSpec((tm, d_model), row),
            pl.BlockSpec((tm, d_model), row),
            pl.BlockSpec((SUBLANES, 128), lambda i: (0, 0)),
        ],
        out_shape=[
            jax.ShapeDtypeStruct((n, diff_width), BF16),
            jax.ShapeDtypeStruct((n, diff_width), BF16),
            jax.ShapeDtypeStruct((vt_rows, n), BF16),
            jax.ShapeDtypeStruct((n, d_model), BF16),
            jax.ShapeDtypeStruct((n, d_model), BF16),
            jax.ShapeDtypeStruct((SUBLANES, 128), F32),
        ],
        scratch_shapes=[pltpu.VMEM((SUBLANES + tm, conv_width), F32)],
        compiler_params=pltpu.CompilerParams(
            dimension_semantics=("arbitrary",), vmem_limit_bytes=VMEM_LIMIT_BYTES),
    )(x2d, g, w_in, w_vt, b_gate, conv_w, w_a, lq1, lk1, lq2, lk2)


def _attn_kernel(slopes_ref, q_ref, k_ref, vt_ref, lam_ref, g_ref, o_ref,
                 q2_ref, bias_ref, sa_ref, sb_ref, m_ref, acc_ref, *, out_scale):
    hg = pl.program_id(1)
    i = pl.program_id(2)
    last_tile = pl.num_programs(2) - 1
    t = q_ref.shape[0]
    d = DIFF_HEAD_DIM
    vd = DIFF_V_DIM
    hps = q_ref.shape[1] // vd
    slope2 = [slopes_ref[hg * hps + hh] * LOG2E for hh in range(hps)]

    @pl.when(i == 0)
    def _():
        key = lax.broadcasted_iota(jnp.int32, (t, 2 * t), 0)
        qry = lax.broadcasted_iota(jnp.int32, (t, 2 * t), 1)
        qry = jnp.where(qry >= t, qry - t, qry)
        rel = (key - qry).astype(F32)
        for hh in range(hps):
            alibi = rel * slope2[hh]
            bias_ref[hh] = alibi
            bias_ref[hps + hh] = jnp.where(rel <= 0.0, alibi, -jnp.inf)
            bias_ref[2 * hps + hh] = jnp.full((t, 2 * t), -jnp.inf, F32)

    lane = lax.broadcasted_iota(jnp.int32, (t, vd), 1)
    for hh in range(hps):
        q = q_ref[:, hh * vd:(hh + 1) * vd]
        zero = jnp.zeros_like(q)
        q2_ref[hh, 0:t, :] = jnp.where(lane < d, q, zero)
        q2_ref[hh, t:2 * t, :] = jnp.where(lane >= d, q, zero)
        m_ref[hh] = jnp.full((1, 2 * t), -jnp.inf, F32)
        acc_ref[hh] = jnp.zeros((VT_ROWS, 2 * t), F32)

    def tile_start(tile):
        return pl.multiple_of(jnp.minimum(tile, last_tile) * t, t)

    def raw_scores(dst_ref, tile):
        start = tile_start(tile)
        for hh in range(hps):
            dst_ref[hh] = _dot_nt(k_ref[pl.ds(start, t), hh * vd:(hh + 1) * vd], q2_ref[hh])

    def consume(src_ref, tile):
        kind = jnp.where(tile == i, 1, jnp.where(tile > i, 2, 0))
        start = tile_start(tile)
        rel_tiles = ((tile - i) * t).astype(F32)
        for hh in range(hps):
            s = src_ref[hh] + bias_ref[kind * hps + hh]
            off = slope2[hh] * rel_tiles
            m_old = m_ref[hh]
            m_new = jnp.maximum(m_old, jnp.max(s, axis=0, keepdims=True) + off)
            p = jnp.exp2(s - (m_new - off)).astype(BF16)
            m_ref[hh] = m_new
            pv = _dot(vt_ref[hh * VT_ROWS:(hh + 1) * VT_ROWS, pl.ds(start, t)], p)
            acc_ref[hh] = jnp.exp2(m_old - m_new) * acc_ref[hh] + pv

    raw_scores(sa_ref, 0)

    def body(pp, carry):
        t0 = 2 * pp
        raw_scores(sb_ref, t0 + 1)
        consume(sa_ref, t0)
        raw_scores(sa_ref, t0 + 2)
        consume(sb_ref, t0 + 1)
        return carry

    lax.fori_loop(0, (i + 2) // 2, body, 0)

    lam = lam_ref[0:1, 0:1]
    for hh in range(hps):
        on = acc_ref[hh, 0:vd, :] * (1.0 / acc_ref[hh, vd:vd + 1, :])
        o = jnp.transpose(on[:, 0:t] - lam * on[:, t:2 * t])
        o = ((o * _rms_scale(o)) * g_ref[...]) * out_scale
        o_ref[:, hh * vd:(hh + 1) * vd] = o.astype(o_ref.dtype)


def _diff_attention(q, k, vt, lam, subln_g, *, batch, seq, out_scale):
    n, width = q.shape
    heads = width // DIFF_V_DIM
    hps = HEADS_PER_STEP
    t = ATTN_TILE
    assert seq % t == 0 and heads % hps == 0
    nq = seq // t
    slopes = jnp.asarray(
        np.array([2.0 ** (-8.0 * (h + 1) / heads) for h in range(heads)], dtype=np.float32))
    grid_spec = pltpu.PrefetchScalarGridSpec(
        num_scalar_prefetch=1,
        grid=(batch, heads // hps, nq),
        in_specs=[
            pl.BlockSpec((t, hps * DIFF_V_DIM), lambda b, h, i, s: (b * nq + i, h)),
            pl.BlockSpec((seq, hps * DIFF_V_DIM), lambda b, h, i, s: (b, h)),
            pl.BlockSpec((hps * VT_ROWS, seq), lambda b, h, i, s: (h, b)),
            pl.BlockSpec(lam.shape, lambda b, h, i, s: (0, 0)),
            pl.BlockSpec(subln_g.shape, lambda b, h, i, s: (0, 0)),
        ],
        out_specs=pl.BlockSpec((t, hps * DIFF_V_DIM), lambda b, h, i, s: (b * nq + i, h)),
        scratch_shapes=[
            pltpu.VMEM((hps, 2 * t, DIFF_V_DIM), BF16),
            pltpu.VMEM((3 * hps, t, 2 * t), F32),
            pltpu.VMEM((hps, t, 2 * t), F32),
            pltpu.VMEM((hps, t, 2 * t), F32),
            pltpu.VMEM((hps, 1, 2 * t), F32),
            pltpu.VMEM((hps, VT_ROWS, 2 * t), F32),
        ],
    )
    return pl.pallas_call(
        functools.partial(_attn_kernel, out_scale=out_scale),
        name="diff_attn",
        grid_spec=grid_spec,
        out_shape=jax.ShapeDtypeStruct((n, width), BF16),
        compiler_params=pltpu.CompilerParams(
            dimension_semantics=("arbitrary", "arbitrary", "arbitrary"),
            vmem_limit_bytes=VMEM_LIMIT_BYTES),
    )(slopes, q, k, vt, lam, subln_g)


def _out_mlp_kernel(x_ref, gaya_ref, gb_ref, on_ref, w_b_ref, w_o_ref, g_mlp_ref,
                    w1_ref, w2_ref, g_fin_ref, out_ref, *, final_norm):
    y_b = _dot(on_ref[...], w_b_ref[...])
    merged = gaya_ref[...].astype(F32) + gb_ref[...].astype(F32) * y_b
    x1 = x_ref[...] + _dot(merged.astype(BF16), w_o_ref[...])
    xn = ((x1 * _rms_scale(x1)) * g_mlp_ref[...]).astype(BF16)
    d_ff = w1_ref.shape[1]
    acc = x1
    for c in range(0, d_ff, FF_CHUNK):
        hc = jnp.maximum(_dot(xn, w1_ref[:, c:c + FF_CHUNK]), 0.0)
        acc = acc + _dot((hc * hc).astype(BF16), w2_ref[c:c + FF_CHUNK, :])
    if final_norm:
        acc = (acc * _rms_scale(acc)) * g_fin_ref[...]
    out_ref[...] = acc


def _out_mlp(x2d, gaya, gb, on, w_b, w_o, g_mlp, w1, w2, g_fin, *, final_norm):
    n, d_model = x2d.shape
    tm = TOKEN_TILE
    assert n % tm == 0 and w1.shape[1] % FF_CHUNK == 0
    row = lambda i: (i, 0)
    return pl.pallas_call(
        functools.partial(_out_mlp_kernel, final_norm=final_norm),
        name="out_mlp",
        grid=(n // tm,),
        in_specs=[
            pl.BlockSpec((tm, d_model), row),
            pl.BlockSpec((tm, d_model), row),
            pl.BlockSpec((tm, d_model), row),
            pl.BlockSpec((tm, on.shape[1]), row),
            _resident(w_b.shape),
            _resident(w_o.shape),
            _resident(g_mlp.shape),
            _resident(w1.shape),
            _resident(w2.shape),
            _resident(g_fin.shape),
        ],
        out_specs=pl.BlockSpec((tm, d_model), row),
        out_shape=jax.ShapeDtypeStruct((n, d_model), F32),
        compiler_params=pltpu.CompilerParams(
            dimension_semantics=("arbitrary",), vmem_limit_bytes=VMEM_LIMIT_BYTES),
    )(x2d, gaya, gb, on, w_b, w_o, g_mlp, w1, w2, g_fin)


def _lambda_init(layer_idx):
    return 0.8 - 0.6 * math.exp(-0.3 * layer_idx)


def kernel(x, norm_mix_g, w_in, b_gate, conv_w, lambda_q1, lambda_k1, lambda_q2, lambda_k2,
           subln_g, w_a_out, w_b_out, w_o, norm_mlp_g, w_mlp_in, w_mlp_out, norm_final_g):
    batch, seq, d_model = x.shape
    depth = w_in.shape[0]
    conv_width = conv_w.shape[2]
    diff_width = w_b_out.shape[1]
    v0 = 3 * conv_width + 2 * diff_width
    x2d = x.reshape(batch * seq, d_model)
    row = lambda a: a.reshape(1, -1)
    for l in range(depth):
        lam_init = _lambda_init(l)
        w_in_l = w_in[l].astype(BF16)
        w_vt = jnp.transpose(w_in[l][:, v0:v0 + diff_width]).astype(BF16)
        q, k, vt, gaya, gb, lam = _in_proj(
            x2d, row(norm_mix_g[l]), w_in_l, w_vt, row(b_gate[l]), conv_w[l],
            w_a_out[l].astype(BF16), row(lambda_q1[l]), row(lambda_k1[l]),
            row(lambda_q2[l]), row(lambda_k2[l]), seq=seq, lam_init=lam_init)
        on = _diff_attention(q, k, vt, lam, row(subln_g[l]), batch=batch, seq=seq,
                             out_scale=1.0 - lam_init)
        x2d = _out_mlp(
            x2d, gaya, gb, on, w_b_out[l].astype(BF16), w_o[l].astype(BF16),
            row(norm_mlp_g[l]), w_mlp_in[l].astype(BF16), w_mlp_out[l].astype(BF16),
            row(norm_final_g), final_norm=(l == depth - 1))
    return x2d.reshape(batch, seq, d_model)
```

```python
import functools
import math

import jax
import jax.numpy as jnp
import numpy as np
from jax import lax
from jax.experimental import pallas as pl
from jax.experimental.pallas import tpu as pltpu

F32 = jnp.float32
BF16 = jnp.bfloat16

RMS_EPS = 1e-6
CONV_K = 3
DIFF_HEAD_DIM = 64
DIFF_V_DIM = 2 * DIFF_HEAD_DIM

LOG2E = math.log2(math.e)
LANES = 128
SUBLANES = 8
VMEM_LIMIT_BYTES = 56 * 1024 * 1024

TOKEN_TILE = 512
FF_CHUNK = 1024
HEADS_PER_STEP = 4
Q_TILE = 256
KEY_BLOCK = 2 * Q_TILE
POS_SPLIT = 16
SLOPE_PIECES = 3


def _rms_scale(x):
    return lax.rsqrt(jnp.mean(x * x, axis=-1, keepdims=True) + RMS_EPS)


def _dot(a, b):
    return jnp.dot(a, b, preferred_element_type=F32)


def _dot_nt(a, b):
    return lax.dot_general(a, b, (((1,), (1,)), ((), ())), preferred_element_type=F32)


def _resident(shape):
    return pl.BlockSpec(shape, lambda *_: (0,) * len(shape), pipeline_mode=pl.Buffered(1))


def _in_proj_kernel(x_ref, g_ref, w_in_ref, w_vt_ref, b_gate_ref, conv_w_ref, w_a_ref,
                    lq1_ref, lk1_ref, lq2_ref, lk2_ref,
                    q_ref, k_ref, vt_ref, gaya_ref, gb_ref, lam_ref,
                    h_ref, *, tiles_per_seq, conv_width, diff_width, d_model, lam_init):
    i = pl.program_id(0)
    tm = x_ref.shape[0]
    c0 = conv_width
    q0 = 3 * conv_width
    g0 = q0 + 3 * diff_width

    @pl.when(i == 0)
    def _():
        lam = (jnp.exp(jnp.sum(lq1_ref[...] * lk1_ref[...], axis=-1, keepdims=True))
               - jnp.exp(jnp.sum(lq2_ref[...] * lk2_ref[...], axis=-1, keepdims=True))
               + lam_init)
        lam_ref[...] = jnp.broadcast_to(lam, lam_ref.shape)

    x = x_ref[...]
    xn = ((x * _rms_scale(x)) * g_ref[...]).astype(BF16)

    uc = _dot(xn, w_in_ref[:, 0:q0])
    h = uc[:, c0:2 * c0] * uc[:, 2 * c0:3 * c0]

    @pl.when(i % tiles_per_seq == 0)
    def _():
        h_ref[0:SUBLANES, :] = jnp.zeros((SUBLANES, c0), F32)

    h_ref[SUBLANES:SUBLANES + tm, :] = h
    cw = conv_w_ref[...]
    conv = (cw[0:1, :] * h_ref[SUBLANES - 2:SUBLANES - 2 + tm, :]
            + cw[1:2, :] * h_ref[SUBLANES - 1:SUBLANES - 1 + tm, :]
            + cw[2:3, :] * h)
    h_ref[0:SUBLANES, :] = h[tm - SUBLANES:tm, :]
    y_a = _dot((uc[:, 0:c0] * conv).astype(BF16), w_a_ref[...])

    gates = jax.nn.sigmoid(_dot(xn, w_in_ref[:, g0:g0 + 2 * d_model]) + b_gate_ref[...])
    gaya_ref[...] = (gates[:, 0:d_model] * y_a).astype(BF16)
    gb_ref[...] = gates[:, d_model:2 * d_model].astype(BF16)

    scale = DIFF_HEAD_DIM ** -0.5 * LOG2E
    q_ref[...] = (_dot(xn, w_in_ref[:, q0:q0 + diff_width]) * scale).astype(BF16)
    k_ref[...] = _dot(xn, w_in_ref[:, q0 + diff_width:q0 + 2 * diff_width]).astype(BF16)
    vt_ref[...] = _dot_nt(w_vt_ref[...], xn).astype(BF16)


def _in_proj(x2d, g, w_in, w_vt, b_gate, conv_w, w_a, lq1, lk1, lq2, lk2, *, seq, lam_init):
    n, d_model = x2d.shape
    conv_width = conv_w.shape[1]
    diff_width = w_vt.shape[0]
    tm = TOKEN_TILE
    assert seq % tm == 0 and n % seq == 0
    row = lambda i: (i, 0)
    kern = functools.partial(
        _in_proj_kernel, tiles_per_seq=seq // tm, conv_width=conv_width,
        diff_width=diff_width, d_model=d_model, lam_init=lam_init)
    return pl.pallas_call(
        kern,
        name="in_proj",
        grid=(n // tm,),
        in_specs=[
            pl.BlockSpec((tm, d_model), row),
            _resident(g.shape),
            _resident(w_in.shape),
            _resident(w_vt.shape),
            _resident(b_gate.shape),
            _resident(conv_w.shape),
            _resident(w_a.shape),
            _resident(lq1.shape), _resident(lk1.shape), _resident(lq2.shape), _resident(lk2.shape),
        ],
        out_specs=[
            pl.BlockSpec((tm, diff_width), row),
            pl.BlockSpec((tm, diff_width), row),
            pl.BlockSpec((diff_width, tm), lambda i: (0, i)),
            pl.BlockSpec((tm, d_model), row),
            pl.BlockSpec((tm, d_model), row),
            pl.BlockSpec((SUBLANES, LANES), lambda i: (0, 0)),
        ],
        out_shape=[
            jax.ShapeDtypeStruct((n, diff_width), BF16),
            jax.ShapeDtypeStruct((n, diff_width), BF16),
            jax.ShapeDtypeStruct((diff_width, n), BF16),
            jax.ShapeDtypeStruct((n, d_model), BF16),
            jax.ShapeDtypeStruct((n, d_model), BF16),
            jax.ShapeDtypeStruct((SUBLANES, LANES), F32),
        ],
        scratch_shapes=[pltpu.VMEM((SUBLANES + tm, conv_width), F32)],
        compiler_params=pltpu.CompilerParams(
            dimension_semantics=("arbitrary",), vmem_limit_bytes=VMEM_LIMIT_BYTES),
    )(x2d, g, w_in, w_vt, b_gate, conv_w, w_a, lq1, lk1, lq2, lk2)


def _alibi_features(seq, heads):
    pos = np.arange(seq)
    kfeat = np.zeros((seq, LANES), np.float32)
    cfeat = np.zeros((heads, LANES), np.float32)
    as_bf16 = lambda a: a.astype(BF16).astype(np.float32)
    slopes = np.array([2.0 ** (-8.0 * (h + 1) / heads) for h in range(heads)], np.float32)
    rest = (slopes * np.float32(LOG2E)).astype(np.float32)
    for piece in range(SLOPE_PIECES):
        part = as_bf16(rest)
        rest = (rest - part).astype(np.float32)
        kfeat[:, 2 * piece] = POS_SPLIT * (pos // POS_SPLIT)
        kfeat[:, 2 * piece + 1] = pos % POS_SPLIT
        cfeat[:, 2 * piece] = part
        cfeat[:, 2 * piece + 1] = part
    assert seq <= POS_SPLIT * 256 and np.array_equal(as_bf16(kfeat), kfeat)
    return jnp.asarray(kfeat, BF16), jnp.asarray(cfeat, F32)


def _attn_kernel(q_ref, k_ref, kfeat_ref, cfeat_ref, vt_ref, lam_ref, g_ref, o_ref,
                 q2_ref, mask_ref, sx_ref, sy_ref, m_ref, l_ref, acc_ref, *, out_scale):
    hg = pl.program_id(1)
    i = pl.program_id(2)
    t = q_ref.shape[0]
    kb = KEY_BLOCK
    d = DIFF_HEAD_DIM
    vd = DIFF_V_DIM
    hps = q_ref.shape[1] // vd
    nb = i // 2
    parity = i % 2

    @pl.when(i == 0)
    def _():
        key = lax.broadcasted_iota(jnp.int32, (kb, 2 * t), 0)
        qry = lax.broadcasted_iota(jnp.int32, (kb, 2 * t), 1)
        qry = jnp.where(qry >= t, qry - t, qry)
        zero = jnp.zeros((kb, 2 * t), F32)
        mask_ref[0] = jnp.where(key <= qry, zero, -jnp.inf)
        mask_ref[1] = jnp.where(key - t <= qry, zero, -jnp.inf)
        for hh in range(hps):
            row = cfeat_ref[pl.ds(hg * hps + hh, 1), :]
            q2_ref[hh, :, LANES:2 * LANES] = jnp.broadcast_to(row, (2 * t, LANES)).astype(BF16)

    lane = lax.broadcasted_iota(jnp.int32, (t, vd), 1)
    for hh in range(hps):
        q = q_ref[:, hh * vd:(hh + 1) * vd]
        zero = jnp.zeros_like(q)
        q2_ref[hh, 0:t, 0:LANES] = jnp.where(lane < d, q, zero)
        q2_ref[hh, t:2 * t, 0:LANES] = jnp.where(lane >= d, q, zero)
        m_ref[hh] = jnp.full((1, 2 * t), -jnp.inf, F32)
        l_ref[hh] = jnp.zeros((1, 2 * t), F32)
        acc_ref[hh] = jnp.zeros((vd, 2 * t), F32)

    def block_start(b):
        return pl.multiple_of(b * kb, kb)

    def raw_scores(dst_ref, b):
        start = block_start(b)
        feat = kfeat_ref[pl.ds(start, kb), :]
        for hh in range(hps):
            keys = jnp.concatenate([k_ref[pl.ds(start, kb), hh * vd:(hh + 1) * vd], feat], axis=1)
            dst_ref[hh] = _dot_nt(keys, q2_ref[hh])

    def consume(src_ref, b, masked):
        start = block_start(b)
        for hh in range(hps):
            s = src_ref[hh]
            if masked:
                s = s + mask_ref[parity]
            m_old = m_ref[hh]
            m_new = jnp.maximum(m_old, jnp.max(s, axis=0, keepdims=True))
            alpha = jnp.exp2(m_old - m_new)
            p = jnp.exp2(s - m_new)
            m_ref[hh] = m_new
            l_ref[hh] = alpha * l_ref[hh] + jnp.sum(p, axis=0, keepdims=True)
            pv = _dot(vt_ref[hh * vd:(hh + 1) * vd, pl.ds(start, kb)], p.astype(BF16))
            acc_ref[hh] = alpha * acc_ref[hh] + pv

    raw_scores(sx_ref, 0)

    def body(pair, carry):
        b = 2 * pair
        raw_scores(sy_ref, b + 1)
        consume(sx_ref, b, False)
        raw_scores(sx_ref, b + 2)
        consume(sy_ref, b + 1, False)
        return carry

    lax.fori_loop(0, nb // 2, body, 0)

    @pl.when(nb % 2 == 1)
    def _():
        raw_scores(sy_ref, nb)
        consume(sx_ref, nb - 1, False)
        consume(sy_ref, nb, True)

    @pl.when(nb % 2 == 0)
    def _():
        consume(sx_ref, nb, True)

    lam = lam_ref[0:1, 0:1]
    for hh in range(hps):
        on = acc_ref[hh] * (1.0 / l_ref[hh])
        o = jnp.transpose(on[:, 0:t] - lam * on[:, t:2 * t])
        o = ((o * _rms_scale(o)) * g_ref[...]) * out_scale
        o_ref[:, hh * vd:(hh + 1) * vd] = o.astype(o_ref.dtype)


def _diff_attention(q, k, vt, lam, subln_g, *, batch, seq, out_scale):
    n, width = q.shape
    heads = width // DIFF_V_DIM
    hps = HEADS_PER_STEP
    t = Q_TILE
    assert seq % KEY_BLOCK == 0 and heads % hps == 0 and DIFF_V_DIM == LANES
    nq = seq // t
    kfeat, cfeat = _alibi_features(seq, heads)
    scores = pltpu.VMEM((hps, KEY_BLOCK, 2 * t), F32)
    return pl.pallas_call(
        functools.partial(_attn_kernel, out_scale=out_scale),
        name="diff_attn",
        grid=(batch, heads // hps, nq),
        in_specs=[
            pl.BlockSpec((t, hps * DIFF_V_DIM), lambda b, h, i: (b * nq + i, h)),
            pl.BlockSpec((seq, hps * DIFF_V_DIM), lambda b, h, i: (b, h)),
            _resident(kfeat.shape),
            _resident(cfeat.shape),
            pl.BlockSpec((hps * DIFF_V_DIM, seq), lambda b, h, i: (h, b)),
            _resident(lam.shape),
            _resident(subln_g.shape),
        ],
        out_specs=pl.BlockSpec((t, hps * DIFF_V_DIM), lambda b, h, i: (b * nq + i, h)),
        out_shape=jax.ShapeDtypeStruct((n, width), BF16),
        scratch_shapes=[
            pltpu.VMEM((hps, 2 * t, 2 * LANES), BF16),
            pltpu.VMEM((2, KEY_BLOCK, 2 * t), F32),
            scores,
            scores,
            pltpu.VMEM((hps, 1, 2 * t), F32),
            pltpu.VMEM((hps, 1, 2 * t), F32),
            pltpu.VMEM((hps, DIFF_V_DIM, 2 * t), F32),
        ],
        compiler_params=pltpu.CompilerParams(
            dimension_semantics=("arbitrary", "arbitrary", "arbitrary"),
            vmem_limit_bytes=VMEM_LIMIT_BYTES),
    )(q, k, kfeat, cfeat, vt, lam, subln_g)


def _out_mlp_kernel(x_ref, gaya_ref, gb_ref, on_ref, w_b_ref, w_o_ref, g_mlp_ref,
                    w1_ref, w2_ref, g_fin_ref, out_ref, *, final_norm):
    y_b = _dot(on_ref[...], w_b_ref[...])
    merged = gaya_ref[...].astype(F32) + gb_ref[...].astype(F32) * y_b
    x1 = x_ref[...] + _dot(merged.astype(BF16), w_o_ref[...])
    xn = ((x1 * _rms_scale(x1)) * g_mlp_ref[...]).astype(BF16)
    d_ff = w1_ref.shape[1]
    acc = x1
    for c in range(0, d_ff, FF_CHUNK):
        hc = jnp.maximum(_dot(xn, w1_ref[:, c:c + FF_CHUNK]), 0.0)
        acc = acc + _dot((hc * hc).astype(BF16), w2_ref[c:c + FF_CHUNK, :])
    if final_norm:
        acc = (acc * _rms_scale(acc)) * g_fin_ref[...]
    out_ref[...] = acc


def _out_mlp(x2d, gaya, gb, on, w_b, w_o, g_mlp, w1, w2, g_fin, *, final_norm):
    n, d_model = x2d.shape
    tm = TOKEN_TILE
    assert n % tm == 0 and w1.shape[1] % FF_CHUNK == 0
    row = lambda i: (i, 0)
    return pl.pallas_call(
        functools.partial(_out_mlp_kernel, final_norm=final_norm),
        name="out_mlp",
        grid=(n // tm,),
        in_specs=[
            pl.BlockSpec((tm, d_model), row),
            pl.BlockSpec((tm, d_model), row),
            pl.BlockSpec((tm, d_model), row),
            pl.BlockSpec((tm, on.shape[1]), row),
            _resident(w_b.shape),
            _resident(w_o.shape),
            _resident(g_mlp.shape),
            _resident(w1.shape),
            _resident(w2.shape),
            _resident(g_fin.shape),
        ],
        out_specs=pl.BlockSpec((tm, d_model), row),
        out_shape=jax.ShapeDtypeStruct((n, d_model), F32),
        compiler_params=pltpu.CompilerParams(
            dimension_semantics=("arbitrary",), vmem_limit_bytes=VMEM_LIMIT_BYTES),
    )(x2d, gaya, gb, on, w_b, w_o, g_mlp, w1, w2, g_fin)


def _lambda_init(layer_idx):
    return 0.8 - 0.6 * math.exp(-0.3 * layer_idx)


def kernel(x, norm_mix_g, w_in, b_gate, conv_w, lambda_q1, lambda_k1, lambda_q2, lambda_k2,
           subln_g, w_a_out, w_b_out, w_o, norm_mlp_g, w_mlp_in, w_mlp_out, norm_final_g):
    batch, seq, d_model = x.shape
    depth = w_in.shape[0]
    conv_width = conv_w.shape[2]
    diff_width = w_b_out.shape[1]
    v0 = 3 * conv_width + 2 * diff_width
    x2d = x.reshape(batch * seq, d_model)
    row = lambda a: a.reshape(1, -1)
    for l in range(depth):
        lam_init = _lambda_init(l)
        w_in_l = w_in[l].astype(BF16)
        w_vt = jnp.transpose(w_in[l][:, v0:v0 + diff_width]).astype(BF16)
        q, k, vt, gaya, gb, lam = _in_proj(
            x2d, row(norm_mix_g[l]), w_in_l, w_vt, row(b_gate[l]), conv_w[l],
            w_a_out[l].astype(BF16), row(lambda_q1[l]), row(lambda_k1[l]),
            row(lambda_q2[l]), row(lambda_k2[l]), seq=seq, lam_init=lam_init)
        on = _diff_attention(q, k, vt, lam, row(subln_g[l]), batch=batch, seq=seq,
                             out_scale=1.0 - lam_init)
        x2d = _out_mlp(
            x2d, gaya, gb, on, w_b_out[l].astype(BF16), w_o[l].astype(BF16),
            row(norm_mlp_g[l]), w_mlp_in[l].astype(BF16), w_mlp_out[l].astype(BF16),
            row(norm_final_g), final_norm=(l == depth - 1))
    return x2d.reshape(batch, seq, d_model)
```

```python
import functools
import math

import jax
import jax.numpy as jnp
import numpy as np
from jax import lax
from jax.experimental import pallas as pl
from jax.experimental.pallas import tpu as pltpu

F32 = jnp.float32
BF16 = jnp.bfloat16

RMS_EPS = 1e-6
CONV_K = 3
DIFF_HEAD_DIM = 64
DIFF_V_DIM = 2 * DIFF_HEAD_DIM

LOG2E = math.log2(math.e)
LANES = 128
SUM_ROWS = 16
VT_ROWS = DIFF_V_DIM + SUM_ROWS
SUBLANES = 8
VMEM_LIMIT_BYTES = 56 * 1024 * 1024

TOKEN_TILE = 512
FF_CHUNK = 1024
HEADS_PER_STEP = 4
Q_TILE = 256
KEY_BLOCK = 2 * Q_TILE
POS_SPLIT = 16
SLOPE_PIECES = 3


def _rms_scale(x):
    return lax.rsqrt(jnp.mean(x * x, axis=-1, keepdims=True) + RMS_EPS)


def _dot(a, b):
    return jnp.dot(a, b, preferred_element_type=F32)


def _dot_nt(a, b):
    return lax.dot_general(a, b, (((1,), (1,)), ((), ())), preferred_element_type=F32)


def _resident(shape):
    return pl.BlockSpec(shape, lambda *_: (0,) * len(shape), pipeline_mode=pl.Buffered(1))


def _in_proj_kernel(x_ref, g_ref, w_in_ref, w_vt_ref, b_gate_ref, conv_w_ref, w_a_ref,
                    lq1_ref, lk1_ref, lq2_ref, lk2_ref,
                    q_ref, k_ref, vt_ref, gaya_ref, gb_ref, lam_ref,
                    h_ref, *, tiles_per_seq, conv_width, diff_width, d_model, lam_init):
    i = pl.program_id(0)
    tm = x_ref.shape[0]
    c0 = conv_width
    q0 = 3 * conv_width
    g0 = q0 + 3 * diff_width

    @pl.when(i == 0)
    def _():
        lam = (jnp.exp(jnp.sum(lq1_ref[...] * lk1_ref[...], axis=-1, keepdims=True))
               - jnp.exp(jnp.sum(lq2_ref[...] * lk2_ref[...], axis=-1, keepdims=True))
               + lam_init)
        lam_ref[...] = jnp.broadcast_to(lam, lam_ref.shape)

    x = x_ref[...]
    xn = ((x * _rms_scale(x)) * g_ref[...]).astype(BF16)

    uc = _dot(xn, w_in_ref[:, 0:q0])
    h = uc[:, c0:2 * c0] * uc[:, 2 * c0:3 * c0]

    @pl.when(i % tiles_per_seq == 0)
    def _():
        h_ref[0:SUBLANES, :] = jnp.zeros((SUBLANES, c0), F32)

    h_ref[SUBLANES:SUBLANES + tm, :] = h
    cw = conv_w_ref[...]
    conv = (cw[0:1, :] * h_ref[SUBLANES - 2:SUBLANES - 2 + tm, :]
            + cw[1:2, :] * h_ref[SUBLANES - 1:SUBLANES - 1 + tm, :]
            + cw[2:3, :] * h)
    h_ref[0:SUBLANES, :] = h[tm - SUBLANES:tm, :]
    y_a = _dot((uc[:, 0:c0] * conv).astype(BF16), w_a_ref[...])

    gates = jax.nn.sigmoid(_dot(xn, w_in_ref[:, g0:g0 + 2 * d_model]) + b_gate_ref[...])
    gaya_ref[...] = (gates[:, 0:d_model] * y_a).astype(BF16)
    gb_ref[...] = gates[:, d_model:2 * d_model].astype(BF16)

    scale = DIFF_HEAD_DIM ** -0.5 * LOG2E
    q_ref[...] = (_dot(xn, w_in_ref[:, q0:q0 + diff_width]) * scale).astype(BF16)
    k_ref[...] = _dot(xn, w_in_ref[:, q0 + diff_width:q0 + 2 * diff_width]).astype(BF16)
    vt = _dot_nt(w_vt_ref[...], xn).astype(BF16)
    for hh in range(diff_width // DIFF_V_DIM):
        r0 = hh * VT_ROWS
        vt_ref[r0:r0 + DIFF_V_DIM, :] = vt[hh * DIFF_V_DIM:(hh + 1) * DIFF_V_DIM, :]
        vt_ref[r0 + DIFF_V_DIM:r0 + VT_ROWS, :] = jnp.ones((SUM_ROWS, tm), BF16)


def _in_proj(x2d, g, w_in, w_vt, b_gate, conv_w, w_a, lq1, lk1, lq2, lk2, *, seq, lam_init):
    n, d_model = x2d.shape
    conv_width = conv_w.shape[1]
    diff_width = w_vt.shape[0]
    tm = TOKEN_TILE
    assert seq % tm == 0 and n % seq == 0
    vt_rows = (diff_width // DIFF_V_DIM) * VT_ROWS
    row = lambda i: (i, 0)
    kern = functools.partial(
        _in_proj_kernel, tiles_per_seq=seq // tm, conv_width=conv_width,
        diff_width=diff_width, d_model=d_model, lam_init=lam_init)
    return pl.pallas_call(
        kern,
        name="in_proj",
        grid=(n // tm,),
        in_specs=[
            pl.BlockSpec((tm, d_model), row),
            _resident(g.shape),
            _resident(w_in.shape),
            _resident(w_vt.shape),
            _resident(b_gate.shape),
            _resident(conv_w.shape),
            _resident(w_a.shape),
            _resident(lq1.shape), _resident(lk1.shape), _resident(lq2.shape), _resident(lk2.shape),
        ],
        out_specs=[
            pl.BlockSpec((tm, diff_width), row),
            pl.BlockSpec((tm, diff_width), row),
            pl.BlockSpec((vt_rows, tm), lambda i: (0, i)),
            pl.BlockSpec((tm, d_model), row),
            pl.BlockSpec((tm, d_model), row),
            pl.BlockSpec((SUBLANES, LANES), lambda i: (0, 0)),
        ],
        out_shape=[
            jax.ShapeDtypeStruct((n, diff_width), BF16),
            jax.ShapeDtypeStruct((n, diff_width), BF16),
            jax.ShapeDtypeStruct((vt_rows, n), BF16),
            jax.ShapeDtypeStruct((n, d_model), BF16),
            jax.ShapeDtypeStruct((n, d_model), BF16),
            jax.ShapeDtypeStruct((SUBLANES, LANES), F32),
        ],
        scratch_shapes=[pltpu.VMEM((SUBLANES + tm, conv_width), F32)],
        compiler_params=pltpu.CompilerParams(
            dimension_semantics=("arbitrary",), vmem_limit_bytes=VMEM_LIMIT_BYTES),
    )(x2d, g, w_in, w_vt, b_gate, conv_w, w_a, lq1, lk1, lq2, lk2)


def _alibi_features(seq, heads):
    pos = np.arange(seq)
    kfeat = np.zeros((seq, LANES), np.float32)
    cfeat = np.zeros((heads, LANES), np.float32)
    as_bf16 = lambda a: a.astype(BF16).astype(np.float32)
    slopes = np.array([2.0 ** (-8.0 * (h + 1) / heads) for h in range(heads)], np.float32)
    rest = (slopes * np.float32(LOG2E)).astype(np.float32)
    for piece in range(SLOPE_PIECES):
        part = as_bf16(rest)
        rest = (rest - part).astype(np.float32)
        kfeat[:, 2 * piece] = POS_SPLIT * (pos // POS_SPLIT)
        kfeat[:, 2 * piece + 1] = pos % POS_SPLIT
        cfeat[:, 2 * piece] = part
        cfeat[:, 2 * piece + 1] = part
    assert seq <= POS_SPLIT * 256 and np.array_equal(as_bf16(kfeat), kfeat)
    return jnp.asarray(kfeat, BF16), jnp.asarray(cfeat, F32)


def _attn_kernel(q_ref, k_ref, kfeat_ref, cfeat_ref, vt_ref, lam_ref, g_ref, o_ref,
                 q2_ref, mask_ref, s_ref, p_ref, m_ref, alpha_ref, acc_ref, *, out_scale):
    hg = pl.program_id(1)
    i = pl.program_id(2)
    last_tile = pl.num_programs(2) - 1
    t = Q_TILE
    kb = KEY_BLOCK
    d = DIFF_HEAD_DIM
    vd = DIFF_V_DIM
    hps = q2_ref.shape[1]
    tail = hps - 1
    nb = i // 2
    parity = i % 2

    def build_q2(slot, tile):
        rows = pl.ds(pl.multiple_of(tile * t, t), t)
        lane = lax.broadcasted_iota(jnp.int32, (t, vd), 1)
        for hh in range(hps):
            q = q_ref[rows, hh * vd:(hh + 1) * vd]
            zero = jnp.zeros_like(q)
            q2_ref[slot, hh, 0:t, 0:LANES] = jnp.where(lane < d, q, zero)
            q2_ref[slot, hh, t:2 * t, 0:LANES] = jnp.where(lane >= d, q, zero)

    def keys_of(b):
        return pl.ds(pl.multiple_of(b * kb, kb), kb)

    def qk(hh, b, slot):
        keys = jnp.concatenate(
            [k_ref[keys_of(b), hh * vd:(hh + 1) * vd], kfeat_ref[keys_of(b), :]], axis=1)
        s_ref[hh] = _dot_nt(keys, q2_ref[slot, hh])

    def pv(hh, b):
        r = _dot(vt_ref[hh * VT_ROWS:(hh + 1) * VT_ROWS, keys_of(b)], p_ref[hh])
        acc_ref[hh] = alpha_ref[hh] * acc_ref[hh] + r

    def softmax(hh, masked):
        s = s_ref[hh]
        if masked:
            s = s + mask_ref[parity]
        m_old = m_ref[hh]
        m_new = jnp.maximum(m_old, jnp.max(s, axis=0, keepdims=True))
        alpha_ref[hh] = jnp.exp2(m_old - m_new)
        m_ref[hh] = m_new
        p_ref[hh] = jnp.exp2(s - m_new).astype(BF16)

    @pl.when(i == 0)
    def _():
        key = lax.broadcasted_iota(jnp.int32, (kb, 2 * t), 0)
        qry = lax.broadcasted_iota(jnp.int32, (kb, 2 * t), 1)
        qry = jnp.where(qry >= t, qry - t, qry)
        zero = jnp.zeros((kb, 2 * t), F32)
        mask_ref[0] = jnp.where(key <= qry, zero, -jnp.inf)
        mask_ref[1] = jnp.where(key - t <= qry, zero, -jnp.inf)
        for hh in range(hps):
            row = cfeat_ref[pl.ds(hg * hps + hh, 1), :]
            feat = jnp.broadcast_to(row, (2 * t, LANES)).astype(BF16)
            q2_ref[0, hh, :, LANES:2 * LANES] = feat
            q2_ref[1, hh, :, LANES:2 * LANES] = feat
        build_q2(0, 0)
        for hh in range(tail):
            qk(hh, 0, 0)

    for hh in range(hps):
        m_ref[hh] = jnp.full((1, 2 * t), -jnp.inf, F32)
        acc_ref[hh] = jnp.zeros((VT_ROWS, 2 * t), F32)
    p_ref[tail] = jnp.zeros((kb, 2 * t), BF16)
    alpha_ref[tail] = jnp.zeros((1, 2 * t), F32)

    def body(b, carry):
        pv(tail, jnp.maximum(b - 1, 0))
        qk(tail, b, parity)
        softmax(0, False)
        for hh in range(1, hps):
            pv(hh - 1, b)
            qk(hh - 1, b + 1, parity)
            softmax(hh, False)
        return carry

    lax.fori_loop(0, nb, body, 0)

    pv(tail, jnp.maximum(nb - 1, 0))
    qk(tail, nb, parity)
    softmax(0, True)
    build_q2(1 - parity, jnp.minimum(i + 1, last_tile))
    for hh in range(1, hps):
        pv(hh - 1, nb)
        qk(hh - 1, 0, 1 - parity)
        softmax(hh, True)
    pv(tail, nb)

    lam = lam_ref[0:1, 0:1]
    for hh in range(hps):
        on = acc_ref[hh, 0:vd, :] * (1.0 / acc_ref[hh, vd:vd + 1, :])
        o = jnp.transpose(on[:, 0:t] - lam * on[:, t:2 * t])
        o = ((o * _rms_scale(o)) * g_ref[...]) * out_scale
        o_ref[:, hh * vd:(hh + 1) * vd] = o.astype(o_ref.dtype)


def _diff_attention(q, k, vt, lam, subln_g, *, batch, seq, out_scale):
    n, width = q.shape
    heads = width // DIFF_V_DIM
    hps = HEADS_PER_STEP
    t = Q_TILE
    assert seq % KEY_BLOCK == 0 and heads % hps == 0 and DIFF_V_DIM == LANES
    nq = seq // t
    kfeat, cfeat = _alibi_features(seq, heads)
    return pl.pallas_call(
        functools.partial(_attn_kernel, out_scale=out_scale),
        name="diff_attn",
        grid=(batch, heads // hps, nq),
        in_specs=[
            pl.BlockSpec((seq, hps * DIFF_V_DIM), lambda b, h, i: (b, h)),
            pl.BlockSpec((seq, hps * DIFF_V_DIM), lambda b, h, i: (b, h)),
            _resident(kfeat.shape),
            _resident(cfeat.shape),
            pl.BlockSpec((hps * VT_ROWS, seq), lambda b, h, i: (h, b)),
            _resident(lam.shape),
            _resident(subln_g.shape),
        ],
        out_specs=pl.BlockSpec((t, hps * DIFF_V_DIM), lambda b, h, i: (b * nq + i, h)),
        out_shape=jax.ShapeDtypeStruct((n, width), BF16),
        scratch_shapes=[
            pltpu.VMEM((2, hps, 2 * t, 2 * LANES), BF16),
            pltpu.VMEM((2, KEY_BLOCK, 2 * t), F32),
            pltpu.VMEM((hps, KEY_BLOCK, 2 * t), F32),
            pltpu.VMEM((hps, KEY_BLOCK, 2 * t), BF16),
            pltpu.VMEM((hps, 1, 2 * t), F32),
            pltpu.VMEM((hps, 1, 2 * t), F32),
            pltpu.VMEM((hps, VT_ROWS, 2 * t), F32),
        ],
        compiler_params=pltpu.CompilerParams(
            dimension_semantics=("arbitrary", "arbitrary", "arbitrary"),
            vmem_limit_bytes=VMEM_LIMIT_BYTES),
    )(q, k, kfeat, cfeat, vt, lam, subln_g)


def _out_mlp_kernel(x_ref, gaya_ref, gb_ref, on_ref, w_b_ref, w_o_ref, g_mlp_ref,
                    w1_ref, w2_ref, g_fin_ref, out_ref, *, final_norm):
    y_b = _dot(on_ref[...], w_b_ref[...])
    merged = gaya_ref[...].astype(F32) + gb_ref[...].astype(F32) * y_b
    x1 = x_ref[...] + _dot(merged.astype(BF16), w_o_ref[...])
    xn = ((x1 * _rms_scale(x1)) * g_mlp_ref[...]).astype(BF16)
    d_ff = w1_ref.shape[1]
    acc = x1
    for c in range(0, d_ff, FF_CHUNK):
        hc = jnp.maximum(_dot(xn, w1_ref[:, c:c + FF_CHUNK]), 0.0)
        acc = acc + _dot((hc * hc).astype(BF16), w2_ref[c:c + FF_CHUNK, :])
    if final_norm:
        acc = (acc * _rms_scale(acc)) * g_fin_ref[...]
    out_ref[...] = acc


def _out_mlp(x2d, gaya, gb, on, w_b, w_o, g_mlp, w1, w2, g_fin, *, final_norm):
    n, d_model = x2d.shape
    tm = TOKEN_TILE
    assert n % tm == 0 and w1.shape[1] % FF_CHUNK == 0
    row = lambda i: (i, 0)
    return pl.pallas_call(
        functools.partial(_out_mlp_kernel, final_norm=final_norm),
        name="out_mlp",
        grid=(n // tm,),
        in_specs=[
            pl.BlockSpec((tm, d_model), row),
            pl.BlockSpec((tm, d_model), row),
            pl.BlockSpec((tm, d_model), row),
            pl.BlockSpec((tm, on.shape[1]), row),
            _resident(w_b.shape),
            _resident(w_o.shape),
            _resident(g_mlp.shape),
            _resident(w1.shape),
            _resident(w2.shape),
            _resident(g_fin.shape),
        ],
        out_specs=pl.BlockSpec((tm, d_model), row),
        out_shape=jax.ShapeDtypeStruct((n, d_model), F32),
        compiler_params=pltpu.CompilerParams(
            dimension_semantics=("arbitrary",), vmem_limit_bytes=VMEM_LIMIT_BYTES),
    )(x2d, gaya, gb, on, w_b, w_o, g_mlp, w1, w2, g_fin)


def _lambda_init(layer_idx):
    return 0.8 - 0.6 * math.exp(-0.3 * layer_idx)


def kernel(x, norm_mix_g, w_in, b_gate, conv_w, lambda_q1, lambda_k1, lambda_q2, lambda_k2,
           subln_g, w_a_out, w_b_out, w_o, norm_mlp_g, w_mlp_in, w_mlp_out, norm_final_g):
    batch, seq, d_model = x.shape
    depth = w_in.shape[0]
    conv_width = conv_w.shape[2]
    diff_width = w_b_out.shape[1]
    v0 = 3 * conv_width + 2 * diff_width
    x2d = x.reshape(batch * seq, d_model)
    row = lambda a: a.reshape(1, -1)
    for l in range(depth):
        lam_init = _lambda_init(l)
        w_in_l = w_in[l].astype(BF16)
        w_vt = jnp.transpose(w_in[l][:, v0:v0 + diff_width]).astype(BF16)
        q, k, vt, gaya, gb, lam = _in_proj(
            x2d, row(norm_mix_g[l]), w_in_l, w_vt, row(b_gate[l]), conv_w[l],
            w_a_out[l].astype(BF16), row(lambda_q1[l]), row(lambda_k1[l]),
            row(lambda_q2[l]), row(lambda_k2[l]), seq=seq, lam_init=lam_init)
        on = _diff_attention(q, k, vt, lam, row(subln_g[l]), batch=batch, seq=seq,
                             out_scale=1.0 - lam_init)
        x2d = _out_mlp(
            x2d, gaya, gb, on, w_b_out[l].astype(BF16), w_o[l].astype(BF16),
            row(norm_mlp_g[l]), w_mlp_in[l].astype(BF16), w_mlp_out[l].astype(BF16),
            row(norm_final_g), final_norm=(l == depth - 1))
    return x2d.reshape(batch, seq, d_model)
```

```python
import functools
import math

import jax
import jax.numpy as jnp
import numpy as np
from jax import lax
from jax.experimental import pallas as pl
from jax.experimental.pallas import tpu as pltpu

F32 = jnp.float32
BF16 = jnp.bfloat16

RMS_EPS = 1e-6
CONV_K = 3
DIFF_HEAD_DIM = 64
DIFF_V_DIM = 2 * DIFF_HEAD_DIM

LOG2E = math.log2(math.e)
LANES = 128
SUBLANES = 8
VMEM_LIMIT_BYTES = 56 * 1024 * 1024

TOKEN_TILE = 512
FF_CHUNK = 1024
HEADS_PER_STEP = 4
Q_TILE = 256
KEY_BLOCK = 2 * Q_TILE
POS_SPLIT = 16
SLOPE_PIECES = 3


def _rms_scale(x):
    return lax.rsqrt(jnp.mean(x * x, axis=-1, keepdims=True) + RMS_EPS)


def _dot(a, b):
    return jnp.dot(a, b, preferred_element_type=F32)


def _dot_nt(a, b):
    return lax.dot_general(a, b, (((1,), (1,)), ((), ())), preferred_element_type=F32)


def _resident(shape):
    return pl.BlockSpec(shape, lambda *_: (0,) * len(shape), pipeline_mode=pl.Buffered(1))


def _in_proj_kernel(x_ref, g_ref, w_in_ref, b_gate_ref, conv_w_ref, w_a_ref,
                    lq1_ref, lk1_ref, lq2_ref, lk2_ref,
                    q_ref, k_ref, vt_ref, gaya_ref, gb_ref, lam_ref,
                    h_ref, w_vt_ref, *, tiles_per_seq, conv_width, diff_width, d_model, lam_init):
    i = pl.program_id(0)
    tm = x_ref.shape[0]
    c0 = conv_width
    q0 = 3 * conv_width
    g0 = q0 + 3 * diff_width

    @pl.when(i == 0)
    def _():
        lam = (jnp.exp(jnp.sum(lq1_ref[...] * lk1_ref[...], axis=-1, keepdims=True))
               - jnp.exp(jnp.sum(lq2_ref[...] * lk2_ref[...], axis=-1, keepdims=True))
               + lam_init)
        lam_ref[...] = jnp.broadcast_to(lam, lam_ref.shape)
        v0 = q0 + 2 * diff_width
        for c in range(0, diff_width, LANES):
            w_vt_ref[c:c + LANES, :] = jnp.transpose(
                w_in_ref[:, v0 + c:v0 + c + LANES].astype(F32)).astype(BF16)

    x = x_ref[...]
    xn = ((x * _rms_scale(x)) * g_ref[...]).astype(BF16)

    uc = _dot(xn, w_in_ref[:, 0:q0])
    h = uc[:, c0:2 * c0] * uc[:, 2 * c0:3 * c0]

    @pl.when(i % tiles_per_seq == 0)
    def _():
        h_ref[0:SUBLANES, :] = jnp.zeros((SUBLANES, c0), F32)

    h_ref[SUBLANES:SUBLANES + tm, :] = h
    cw = conv_w_ref[...]
    conv = (cw[0:1, :] * h_ref[SUBLANES - 2:SUBLANES - 2 + tm, :]
            + cw[1:2, :] * h_ref[SUBLANES - 1:SUBLANES - 1 + tm, :]
            + cw[2:3, :] * h)
    h_ref[0:SUBLANES, :] = h[tm - SUBLANES:tm, :]
    y_a = _dot((uc[:, 0:c0] * conv).astype(BF16), w_a_ref[...])

    gates = jax.nn.sigmoid(_dot(xn, w_in_ref[:, g0:g0 + 2 * d_model]) + b_gate_ref[...])
    gaya_ref[...] = (gates[:, 0:d_model] * y_a).astype(BF16)
    gb_ref[...] = gates[:, d_model:2 * d_model].astype(BF16)

    scale = DIFF_HEAD_DIM ** -0.5 * LOG2E
    q_ref[...] = (_dot(xn, w_in_ref[:, q0:q0 + diff_width]) * scale).astype(BF16)
    k_ref[...] = _dot(xn, w_in_ref[:, q0 + diff_width:q0 + 2 * diff_width]).astype(BF16)
    vt_ref[...] = _dot_nt(w_vt_ref[...], xn).astype(BF16)


def _in_proj(x2d, g, w_in, b_gate, conv_w, w_a, lq1, lk1, lq2, lk2, *, seq, lam_init):
    n, d_model = x2d.shape
    conv_width = conv_w.shape[1]
    diff_width = (w_in.shape[1] - 3 * conv_width - 2 * d_model) // 3
    tm = TOKEN_TILE
    assert seq % tm == 0 and n % seq == 0
    row = lambda i: (i, 0)
    kern = functools.partial(
        _in_proj_kernel, tiles_per_seq=seq // tm, conv_width=conv_width,
        diff_width=diff_width, d_model=d_model, lam_init=lam_init)
    return pl.pallas_call(
        kern,
        name="in_proj",
        grid=(n // tm,),
        in_specs=[
            pl.BlockSpec((tm, d_model), row),
            _resident(g.shape),
            _resident(w_in.shape),
            _resident(b_gate.shape),
            _resident(conv_w.shape),
            _resident(w_a.shape),
            _resident(lq1.shape), _resident(lk1.shape), _resident(lq2.shape), _resident(lk2.shape),
        ],
        out_specs=[
            pl.BlockSpec((tm, diff_width), row),
            pl.BlockSpec((tm, diff_width), row),
            pl.BlockSpec((diff_width, tm), lambda i: (0, i)),
            pl.BlockSpec((tm, d_model), row),
            pl.BlockSpec((tm, d_model), row),
            pl.BlockSpec((SUBLANES, LANES), lambda i: (0, 0)),
        ],
        out_shape=[
            jax.ShapeDtypeStruct((n, diff_width), BF16),
            jax.ShapeDtypeStruct((n, diff_width), BF16),
            jax.ShapeDtypeStruct((diff_width, n), BF16),
            jax.ShapeDtypeStruct((n, d_model), BF16),
            jax.ShapeDtypeStruct((n, d_model), BF16),
            jax.ShapeDtypeStruct((SUBLANES, LANES), F32),
        ],
        scratch_shapes=[
            pltpu.VMEM((SUBLANES + tm, conv_width), F32),
            pltpu.VMEM((diff_width, d_model), BF16),
        ],
        compiler_params=pltpu.CompilerParams(
            dimension_semantics=("arbitrary",), vmem_limit_bytes=VMEM_LIMIT_BYTES),
    )(x2d, g, w_in, b_gate, conv_w, w_a, lq1, lk1, lq2, lk2)


def _alibi_features(seq, heads):
    pos = np.arange(seq)
    kfeat = np.zeros((seq, LANES), np.float32)
    cfeat = np.zeros((heads, LANES), np.float32)
    as_bf16 = lambda a: a.astype(BF16).astype(np.float32)
    slopes = np.array([2.0 ** (-8.0 * (h + 1) / heads) for h in range(heads)], np.float32)
    rest = (slopes * np.float32(LOG2E)).astype(np.float32)
    for piece in range(SLOPE_PIECES):
        part = as_bf16(rest)
        rest = (rest - part).astype(np.float32)
        kfeat[:, 2 * piece] = POS_SPLIT * (pos // POS_SPLIT)
        kfeat[:, 2 * piece + 1] = pos % POS_SPLIT
        cfeat[:, 2 * piece] = part
        cfeat[:, 2 * piece + 1] = part
    assert seq <= POS_SPLIT * 256 and np.array_equal(as_bf16(kfeat), kfeat)
    return jnp.asarray(kfeat, BF16), jnp.asarray(cfeat, F32)


def _attn_kernel(q_ref, k_ref, kfeat_ref, cfeat_ref, vt_ref, lam_ref, g_ref, o_ref,
                 q2_ref, mask_ref, sx_ref, sy_ref, m_ref, l_ref, acc_ref, *, out_scale):
    hg = pl.program_id(1)
    i = pl.program_id(2)
    last_tile = pl.num_programs(2) - 1
    t = Q_TILE
    kb = KEY_BLOCK
    d = DIFF_HEAD_DIM
    vd = DIFF_V_DIM
    hps = q2_ref.shape[0]
    nb = i // 2
    parity = i % 2

    def build_q2(tile):
        rows = pl.ds(pl.multiple_of(tile * t, t), t)
        lane = lax.broadcasted_iota(jnp.int32, (t, vd), 1)
        for hh in range(hps):
            q = q_ref[rows, hh * vd:(hh + 1) * vd]
            zero = jnp.zeros_like(q)
            q2_ref[hh, 0:t, 0:LANES] = jnp.where(lane < d, q, zero)
            q2_ref[hh, t:2 * t, 0:LANES] = jnp.where(lane >= d, q, zero)

    def block_start(b):
        return pl.multiple_of(b * kb, kb)

    def raw_scores(dst_ref, b):
        start = block_start(b)
        feat = kfeat_ref[pl.ds(start, kb), :]
        for hh in range(hps):
            keys = jnp.concatenate([k_ref[pl.ds(start, kb), hh * vd:(hh + 1) * vd], feat], axis=1)
            dst_ref[hh] = _dot_nt(keys, q2_ref[hh])

    def consume(src_ref, b, masked):
        start = block_start(b)
        for hh in range(hps):
            s = src_ref[hh]
            if masked:
                s = s + mask_ref[parity]
            m_old = m_ref[hh]
            m_new = jnp.maximum(m_old, jnp.max(s, axis=0, keepdims=True))
            alpha = jnp.exp2(m_old - m_new)
            p = jnp.exp2(s - m_new)
            m_ref[hh] = m_new
            l_ref[hh] = alpha * l_ref[hh] + jnp.sum(p, axis=0, keepdims=True)
            pv = _dot(vt_ref[hh * vd:(hh + 1) * vd, pl.ds(start, kb)], p.astype(BF16))
            acc_ref[hh] = alpha * acc_ref[hh] + pv

    @pl.when(i == 0)
    def _():
        key = lax.broadcasted_iota(jnp.int32, (kb, 2 * t), 0)
        qry = lax.broadcasted_iota(jnp.int32, (kb, 2 * t), 1)
        qry = jnp.where(qry >= t, qry - t, qry)
        zero = jnp.zeros((kb, 2 * t), F32)
        mask_ref[0] = jnp.where(key <= qry, zero, -jnp.inf)
        mask_ref[1] = jnp.where(key - t <= qry, zero, -jnp.inf)
        for hh in range(hps):
            row = cfeat_ref[pl.ds(hg * hps + hh, 1), :]
            q2_ref[hh, :, LANES:2 * LANES] = jnp.broadcast_to(row, (2 * t, LANES)).astype(BF16)
        build_q2(0)
        raw_scores(sx_ref, 0)

    for hh in range(hps):
        m_ref[hh] = jnp.full((1, 2 * t), -jnp.inf, F32)
        l_ref[hh] = jnp.zeros((1, 2 * t), F32)
        acc_ref[hh] = jnp.zeros((vd, 2 * t), F32)

    def body(pair, carry):
        b = 2 * pair
        raw_scores(sy_ref, b + 1)
        consume(sx_ref, b, False)
        raw_scores(sx_ref, b + 2)
        consume(sy_ref, b + 1, False)
        return carry

    lax.fori_loop(0, nb // 2, body, 0)

    @pl.when(nb % 2 == 1)
    def _():
        raw_scores(sy_ref, nb)
        consume(sx_ref, nb - 1, False)
        consume(sy_ref, nb, True)

    @pl.when(nb % 2 == 0)
    def _():
        consume(sx_ref, nb, True)

    build_q2(jnp.minimum(i + 1, last_tile))
    raw_scores(sx_ref, 0)

    lam = lam_ref[0:1, 0:1]
    for hh in range(hps):
        on = acc_ref[hh] * (1.0 / l_ref[hh])
        o = jnp.transpose(on[:, 0:t] - lam * on[:, t:2 * t])
        o = ((o * _rms_scale(o)) * g_ref[...]) * out_scale
        o_ref[:, hh * vd:(hh + 1) * vd] = o.astype(o_ref.dtype)


def _diff_attention(q, k, vt, lam, subln_g, *, batch, seq, out_scale):
    n, width = q.shape
    heads = width // DIFF_V_DIM
    hps = HEADS_PER_STEP
    t = Q_TILE
    assert seq % KEY_BLOCK == 0 and heads % hps == 0 and DIFF_V_DIM == LANES
    nq = seq // t
    kfeat, cfeat = _alibi_features(seq, heads)
    scores = pltpu.VMEM((hps, KEY_BLOCK, 2 * t), F32)
    per_head_seq = pl.BlockSpec((seq, hps * DIFF_V_DIM), lambda b, h, i: (b, h))
    return pl.pallas_call(
        functools.partial(_attn_kernel, out_scale=out_scale),
        name="diff_attn",
        grid=(batch, heads // hps, nq),
        in_specs=[
            per_head_seq,
            per_head_seq,
            _resident(kfeat.shape),
            _resident(cfeat.shape),
            pl.BlockSpec((hps * DIFF_V_DIM, seq), lambda b, h, i: (h, b)),
            _resident(lam.shape),
            _resident(subln_g.shape),
        ],
        out_specs=pl.BlockSpec((t, hps * DIFF_V_DIM), lambda b, h, i: (b * nq + i, h)),
        out_shape=jax.ShapeDtypeStruct((n, width), BF16),
        scratch_shapes=[
            pltpu.VMEM((hps, 2 * t, 2 * LANES), BF16),
            pltpu.VMEM((2, KEY_BLOCK, 2 * t), F32),
            scores,
            scores,
            pltpu.VMEM((hps, 1, 2 * t), F32),
            pltpu.VMEM((hps, 1, 2 * t), F32),
            pltpu.VMEM((hps, DIFF_V_DIM, 2 * t), F32),
        ],
        compiler_params=pltpu.CompilerParams(
            dimension_semantics=("arbitrary", "arbitrary", "arbitrary"),
            vmem_limit_bytes=VMEM_LIMIT_BYTES),
    )(q, k, kfeat, cfeat, vt, lam, subln_g)


def _out_mlp_kernel(x_ref, gaya_ref, gb_ref, on_ref, w_b_ref, w_o_ref, g_mlp_ref,
                    w1_ref, w2_ref, g_fin_ref, out_ref, *, final_norm):
    y_b = _dot(on_ref[...], w_b_ref[...])
    merged = gaya_ref[...].astype(F32) + gb_ref[...].astype(F32) * y_b
    x1 = x_ref[...] + _dot(merged.astype(BF16), w_o_ref[...])
    xn = ((x1 * _rms_scale(x1)) * g_mlp_ref[...]).astype(BF16)
    d_ff = w1_ref.shape[1]
    acc = x1
    for c in range(0, d_ff, FF_CHUNK):
        hc = jnp.maximum(_dot(xn, w1_ref[:, c:c + FF_CHUNK]), 0.0)
        acc = acc + _dot((hc * hc).astype(BF16), w2_ref[c:c + FF_CHUNK, :])
    if final_norm:
        acc = (acc * _rms_scale(acc)) * g_fin_ref[...]
    out_ref[...] = acc


def _out_mlp(x2d, gaya, gb, on, w_b, w_o, g_mlp, w1, w2, g_fin, *, final_norm):
    n, d_model = x2d.shape
    tm = TOKEN_TILE
    assert n % tm == 0 and w1.shape[1] % FF_CHUNK == 0
    row = lambda i: (i, 0)
    return pl.pallas_call(
        functools.partial(_out_mlp_kernel, final_norm=final_norm),
        name="out_mlp",
        grid=(n // tm,),
        in_specs=[
            pl.BlockSpec((tm, d_model), row),
            pl.BlockSpec((tm, d_model), row),
            pl.BlockSpec((tm, d_model), row),
            pl.BlockSpec((tm, on.shape[1]), row),
            _resident(w_b.shape),
            _resident(w_o.shape),
            _resident(g_mlp.shape),
            _resident(w1.shape),
            _resident(w2.shape),
            _resident(g_fin.shape),
        ],
        out_specs=pl.BlockSpec((tm, d_model), row),
        out_shape=jax.ShapeDtypeStruct((n, d_model), F32),
        compiler_params=pltpu.CompilerParams(
            dimension_semantics=("arbitrary",), vmem_limit_bytes=VMEM_LIMIT_BYTES),
    )(x2d, gaya, gb, on, w_b, w_o, g_mlp, w1, w2, g_fin)


def _lambda_init(layer_idx):
    return 0.8 - 0.6 * math.exp(-0.3 * layer_idx)


def kernel(x, norm_mix_g, w_in, b_gate, conv_w, lambda_q1, lambda_k1, lambda_q2, lambda_k2,
           subln_g, w_a_out, w_b_out, w_o, norm_mlp_g, w_mlp_in, w_mlp_out, norm_final_g):
    batch, seq, d_model = x.shape
    depth = w_in.shape[0]
    x2d = x.reshape(batch * seq, d_model)
    row = lambda a: a.reshape(1, -1)
    for l in range(depth):
        lam_init = _lambda_init(l)
        q, k, vt, gaya, gb, lam = _in_proj(
            x2d, row(norm_mix_g[l]), w_in[l].astype(BF16), row(b_gate[l]), conv_w[l],
            w_a_out[l].astype(BF16), row(lambda_q1[l]), row(lambda_k1[l]),
            row(lambda_q2[l]), row(lambda_k2[l]), seq=seq, lam_init=lam_init)
        on = _diff_attention(q, k, vt, lam, row(subln_g[l]), batch=batch, seq=seq,
                             out_scale=1.0 - lam_init)
        x2d = _out_mlp(
            x2d, gaya, gb, on, w_b_out[l].astype(BF16), w_o[l].astype(BF16),
            row(norm_mlp_g[l]), w_mlp_in[l].astype(BF16), w_mlp_out[l].astype(BF16),
            row(norm_final_g), final_norm=(l == depth - 1))
    return x2d.reshape(batch, seq, d_model)
```

```python
import functools
import math

import jax
import jax.numpy as jnp
import numpy as np
from jax import lax
from jax.experimental import pallas as pl
from jax.experimental.pallas import tpu as pltpu

F32 = jnp.float32
BF16 = jnp.bfloat16

RMS_EPS = 1e-6
CONV_K = 3
DIFF_HEAD_DIM = 64
DIFF_V_DIM = 2 * DIFF_HEAD_DIM

LOG2E = math.log2(math.e)
LANES = 128
SUBLANES = 8
VMEM_LIMIT_BYTES = 56 * 1024 * 1024

TOKEN_TILE = 512
FF_CHUNK = 1024
HEADS_PER_STEP = 4
Q_TILE = 256
KEY_BLOCK = 2 * Q_TILE
POS_SPLIT = 16
SLOPE_PIECES = 3


def _rms_scale(x):
    return lax.rsqrt(jnp.mean(x * x, axis=-1, keepdims=True) + RMS_EPS)


def _dot(a, b):
    return jnp.dot(a, b, preferred_element_type=F32)


def _dot_nt(a, b):
    return lax.dot_general(a, b, (((1,), (1,)), ((), ())), preferred_element_type=F32)


def _resident(shape):
    return pl.BlockSpec(shape, lambda *_: (0,) * len(shape), pipeline_mode=pl.Buffered(1))


def _in_proj_kernel(x0_ref, x_next_ref, g_ref, w_in_ref, b_gate_ref, conv_w_ref, w_a_ref,
                    lq1_ref, lk1_ref, lq2_ref, lk2_ref,
                    q_ref, k_ref, vt_ref, gaya_ref, gb_ref, lam_ref,
                    h_ref, w_vt_ref, xn_ref, *, tiles_per_seq, conv_width, diff_width, d_model,
                    lam_init):
    i = pl.program_id(0)
    tm = xn_ref.shape[0]
    c0 = conv_width
    q0 = 3 * conv_width
    g0 = q0 + 3 * diff_width

    def normed(x_ref):
        x = x_ref[...]
        return ((x * _rms_scale(x)) * g_ref[...]).astype(BF16)

    @pl.when(i == 0)
    def _():
        lam = (jnp.exp(jnp.sum(lq1_ref[...] * lk1_ref[...], axis=-1, keepdims=True))
               - jnp.exp(jnp.sum(lq2_ref[...] * lk2_ref[...], axis=-1, keepdims=True))
               + lam_init)
        lam_ref[...] = jnp.broadcast_to(lam, lam_ref.shape)
        v0 = q0 + 2 * diff_width
        for c in range(0, diff_width, LANES):
            w_vt_ref[c:c + LANES, :] = jnp.transpose(
                w_in_ref[:, v0 + c:v0 + c + LANES].astype(F32)).astype(BF16)
        xn_ref[...] = normed(x0_ref)

    xn = xn_ref[...]

    uc = _dot(xn, w_in_ref[:, 0:q0])
    h = uc[:, c0:2 * c0] * uc[:, 2 * c0:3 * c0]

    @pl.when(i % tiles_per_seq == 0)
    def _():
        h_ref[0:SUBLANES, :] = jnp.zeros((SUBLANES, c0), F32)

    h_ref[SUBLANES:SUBLANES + tm, :] = h
    cw = conv_w_ref[...]
    conv = (cw[0:1, :] * h_ref[SUBLANES - 2:SUBLANES - 2 + tm, :]
            + cw[1:2, :] * h_ref[SUBLANES - 1:SUBLANES - 1 + tm, :]
            + cw[2:3, :] * h)
    h_ref[0:SUBLANES, :] = h[tm - SUBLANES:tm, :]
    gated = (uc[:, 0:c0] * conv).astype(BF16)

    gates = jax.nn.sigmoid(_dot(xn, w_in_ref[:, g0:g0 + 2 * d_model]) + b_gate_ref[...])
    gb_ref[...] = gates[:, d_model:2 * d_model].astype(BF16)

    scale = DIFF_HEAD_DIM ** -0.5 * LOG2E
    q_ref[...] = (_dot(xn, w_in_ref[:, q0:q0 + diff_width]) * scale).astype(BF16)
    k_ref[...] = _dot(xn, w_in_ref[:, q0 + diff_width:q0 + 2 * diff_width]).astype(BF16)
    vt_ref[...] = _dot_nt(w_vt_ref[...], xn).astype(BF16)

    gaya_ref[...] = (gates[:, 0:d_model] * _dot(gated, w_a_ref[...])).astype(BF16)

    xn_ref[...] = normed(x_next_ref)


def _in_proj(x2d, g, w_in, b_gate, conv_w, w_a, lq1, lk1, lq2, lk2, *, seq, lam_init):
    n, d_model = x2d.shape
    conv_width = conv_w.shape[1]
    diff_width = (w_in.shape[1] - 3 * conv_width - 2 * d_model) // 3
    tm = TOKEN_TILE
    assert seq % tm == 0 and n % seq == 0
    steps = n // tm
    row = lambda i: (i, 0)
    kern = functools.partial(
        _in_proj_kernel, tiles_per_seq=seq // tm, conv_width=conv_width,
        diff_width=diff_width, d_model=d_model, lam_init=lam_init)
    return pl.pallas_call(
        kern,
        name="in_proj",
        grid=(steps,),
        in_specs=[
            pl.BlockSpec((tm, d_model), lambda i: (0, 0)),
            pl.BlockSpec((tm, d_model), lambda i: (jnp.minimum(i + 1, steps - 1), 0)),
            _resident(g.shape),
            _resident(w_in.shape),
            _resident(b_gate.shape),
            _resident(conv_w.shape),
            _resident(w_a.shape),
            _resident(lq1.shape), _resident(lk1.shape), _resident(lq2.shape), _resident(lk2.shape),
        ],
        out_specs=[
            pl.BlockSpec((tm, diff_width), row),
            pl.BlockSpec((tm, diff_width), row),
            pl.BlockSpec((diff_width, tm), lambda i: (0, i)),
            pl.BlockSpec((tm, d_model), row),
            pl.BlockSpec((tm, d_model), row),
            pl.BlockSpec((SUBLANES, LANES), lambda i: (0, 0)),
        ],
        out_shape=[
            jax.ShapeDtypeStruct((n, diff_width), BF16),
            jax.ShapeDtypeStruct((n, diff_width), BF16),
            jax.ShapeDtypeStruct((diff_width, n), BF16),
            jax.ShapeDtypeStruct((n, d_model), BF16),
            jax.ShapeDtypeStruct((n, d_model), BF16),
            jax.ShapeDtypeStruct((SUBLANES, LANES), F32),
        ],
        scratch_shapes=[
            pltpu.VMEM((SUBLANES + tm, conv_width), F32),
            pltpu.VMEM((diff_width, d_model), BF16),
            pltpu.VMEM((tm, d_model), BF16),
        ],
        compiler_params=pltpu.CompilerParams(
            dimension_semantics=("arbitrary",), vmem_limit_bytes=VMEM_LIMIT_BYTES),
    )(x2d, x2d, g, w_in, b_gate, conv_w, w_a, lq1, lk1, lq2, lk2)


def _alibi_features(seq, heads):
    pos = np.arange(seq)
    kfeat = np.zeros((seq, LANES), np.float32)
    cfeat = np.zeros((heads, LANES), np.float32)
    as_bf16 = lambda a: a.astype(BF16).astype(np.float32)
    slopes = np.array([2.0 ** (-8.0 * (h + 1) / heads) for h in range(heads)], np.float32)
    rest = (slopes * np.float32(LOG2E)).astype(np.float32)
    for piece in range(SLOPE_PIECES):
        part = as_bf16(rest)
        rest = (rest - part).astype(np.float32)
        kfeat[:, 2 * piece] = POS_SPLIT * (pos // POS_SPLIT)
        kfeat[:, 2 * piece + 1] = pos % POS_SPLIT
        cfeat[:, 2 * piece] = part
        cfeat[:, 2 * piece + 1] = part
    assert seq <= POS_SPLIT * 256 and np.array_equal(as_bf16(kfeat), kfeat)
    return jnp.asarray(kfeat, BF16), jnp.asarray(cfeat, F32)


def _attn_kernel(q_ref, k_ref, kfeat_ref, cfeat_ref, vt_ref, lam_ref, g_ref, o_ref,
                 q2_ref, mask_ref, sx_ref, sy_ref, m_ref, l_ref, acc_ref, *, out_scale):
    hg = pl.program_id(1)
    i = pl.program_id(2)
    last_tile = pl.num_programs(2) - 1
    t = Q_TILE
    kb = KEY_BLOCK
    d = DIFF_HEAD_DIM
    vd = DIFF_V_DIM
    hps = q2_ref.shape[0]
    nb = i // 2
    parity = i % 2

    def build_q2(tile):
        rows = pl.ds(pl.multiple_of(tile * t, t), t)
        lane = lax.broadcasted_iota(jnp.int32, (t, vd), 1)
        for hh in range(hps):
            q = q_ref[rows, hh * vd:(hh + 1) * vd]
            zero = jnp.zeros_like(q)
            q2_ref[hh, 0:t, 0:LANES] = jnp.where(lane < d, q, zero)
            q2_ref[hh, t:2 * t, 0:LANES] = jnp.where(lane >= d, q, zero)

    def block_start(b):
        return pl.multiple_of(b * kb, kb)

    def raw_scores(dst_ref, b):
        start = block_start(b)
        feat = kfeat_ref[pl.ds(start, kb), :]
        for hh in range(hps):
            keys = jnp.concatenate([k_ref[pl.ds(start, kb), hh * vd:(hh + 1) * vd], feat], axis=1)
            dst_ref[hh] = _dot_nt(keys, q2_ref[hh])

    def consume(src_ref, b, masked):
        start = block_start(b)
        for hh in range(hps):
            s = src_ref[hh]
            if masked:
                s = s + mask_ref[parity]
            m_old = m_ref[hh]
            m_new = jnp.maximum(m_old, jnp.max(s, axis=0, keepdims=True))
            alpha = jnp.exp2(m_old - m_new)
            p = jnp.exp2(s - m_new)
            m_ref[hh] = m_new
            l_ref[hh] = alpha * l_ref[hh] + jnp.sum(p, axis=0, keepdims=True)
            pv = _dot(vt_ref[hh * vd:(hh + 1) * vd, pl.ds(start, kb)], p.astype(BF16))
            acc_ref[hh] = alpha * acc_ref[hh] + pv

    @pl.when(i == 0)
    def _():
        key = lax.broadcasted_iota(jnp.int32, (kb, 2 * t), 0)
        qry = lax.broadcasted_iota(jnp.int32, (kb, 2 * t), 1)
        qry = jnp.where(qry >= t, qry - t, qry)
        zero = jnp.zeros((kb, 2 * t), F32)
        mask_ref[0] = jnp.where(key <= qry, zero, -jnp.inf)
        mask_ref[1] = jnp.where(key - t <= qry, zero, -jnp.inf)
        for hh in range(hps):
            row = cfeat_ref[pl.ds(hg * hps + hh, 1), :]
            q2_ref[hh, :, LANES:2 * LANES] = jnp.broadcast_to(row, (2 * t, LANES)).astype(BF16)
        build_q2(0)
        raw_scores(sx_ref, 0)

    for hh in range(hps):
        m_ref[hh] = jnp.full((1, 2 * t), -jnp.inf, F32)
        l_ref[hh] = jnp.zeros((1, 2 * t), F32)
        acc_ref[hh] = jnp.zeros((vd, 2 * t), F32)

    def body(pair, carry):
        b = 2 * pair
        raw_scores(sy_ref, b + 1)
        consume(sx_ref, b, False)
        raw_scores(sx_ref, b + 2)
        consume(sy_ref, b + 1, False)
        return carry

    lax.fori_loop(0, nb // 2, body, 0)

    @pl.when(nb % 2 == 1)
    def _():
        raw_scores(sy_ref, nb)
        consume(sx_ref, nb - 1, False)
        consume(sy_ref, nb, True)

    @pl.when(nb % 2 == 0)
    def _():
        consume(sx_ref, nb, True)

    build_q2(jnp.minimum(i + 1, last_tile))
    raw_scores(sx_ref, 0)

    lam = lam_ref[0:1, 0:1]
    for hh in range(hps):
        on = acc_ref[hh] * (1.0 / l_ref[hh])
        o = jnp.transpose(on[:, 0:t] - lam * on[:, t:2 * t])
        o = ((o * _rms_scale(o)) * g_ref[...]) * out_scale
        o_ref[:, hh * vd:(hh + 1) * vd] = o.astype(o_ref.dtype)


def _diff_attention(q, k, vt, lam, subln_g, *, batch, seq, out_scale):
    n, width = q.shape
    heads = width // DIFF_V_DIM
    hps = HEADS_PER_STEP
    t = Q_TILE
    assert seq % KEY_BLOCK == 0 and heads % hps == 0 and DIFF_V_DIM == LANES
    nq = seq // t
    kfeat, cfeat = _alibi_features(seq, heads)
    scores = pltpu.VMEM((hps, KEY_BLOCK, 2 * t), F32)
    per_head_seq = pl.BlockSpec((seq, hps * DIFF_V_DIM), lambda b, h, i: (b, h))
    return pl.pallas_call(
        functools.partial(_attn_kernel, out_scale=out_scale),
        name="diff_attn",
        grid=(batch, heads // hps, nq),
        in_specs=[
            per_head_seq,
            per_head_seq,
            _resident(kfeat.shape),
            _resident(cfeat.shape),
            pl.BlockSpec((hps * DIFF_V_DIM, seq), lambda b, h, i: (h, b)),
            _resident(lam.shape),
            _resident(subln_g.shape),
        ],
        out_specs=pl.BlockSpec((t, hps * DIFF_V_DIM), lambda b, h, i: (b * nq + i, h)),
        out_shape=jax.ShapeDtypeStruct((n, width), BF16),
        scratch_shapes=[
            pltpu.VMEM((hps, 2 * t, 2 * LANES), BF16),
            pltpu.VMEM((2, KEY_BLOCK, 2 * t), F32),
            scores,
            scores,
            pltpu.VMEM((hps, 1, 2 * t), F32),
            pltpu.VMEM((hps, 1, 2 * t), F32),
            pltpu.VMEM((hps, DIFF_V_DIM, 2 * t), F32),
        ],
        compiler_params=pltpu.CompilerParams(
            dimension_semantics=("arbitrary", "arbitrary", "arbitrary"),
            vmem_limit_bytes=VMEM_LIMIT_BYTES),
    )(q, k, kfeat, cfeat, vt, lam, subln_g)


def _out_mlp_kernel(x_ref, gaya_ref, gb_ref, on_ref, w_b_ref, w_o_ref, g_mlp_ref,
                    w1_ref, w2_ref, g_fin_ref, out_ref, *, final_norm):
    y_b = _dot(on_ref[...], w_b_ref[...])
    merged = gaya_ref[...].astype(F32) + gb_ref[...].astype(F32) * y_b
    x1 = x_ref[...] + _dot(merged.astype(BF16), w_o_ref[...])
    xn = ((x1 * _rms_scale(x1)) * g_mlp_ref[...]).astype(BF16)
    d_ff = w1_ref.shape[1]
    acc = x1
    for c in range(0, d_ff, FF_CHUNK):
        hc = jnp.maximum(_dot(xn, w1_ref[:, c:c + FF_CHUNK]), 0.0)
        acc = acc + _dot((hc * hc).astype(BF16), w2_ref[c:c + FF_CHUNK, :])
    if final_norm:
        acc = (acc * _rms_scale(acc)) * g_fin_ref[...]
    out_ref[...] = acc


def _out_mlp(x2d, gaya, gb, on, w_b, w_o, g_mlp, w1, w2, g_fin, *, final_norm):
    n, d_model = x2d.shape
    tm = TOKEN_TILE
    assert n % tm == 0 and w1.shape[1] % FF_CHUNK == 0
    row = lambda i: (i, 0)
    return pl.pallas_call(
        functools.partial(_out_mlp_kernel, final_norm=final_norm),
        name="out_mlp",
        grid=(n // tm,),
        in_specs=[
            pl.BlockSpec((tm, d_model), row),
            pl.BlockSpec((tm, d_model), row),
            pl.BlockSpec((tm, d_model), row),
            pl.BlockSpec((tm, on.shape[1]), row),
            _resident(w_b.shape),
            _resident(w_o.shape),
            _resident(g_mlp.shape),
            _resident(w1.shape),
            _resident(w2.shape),
            _resident(g_fin.shape),
        ],
        out_specs=pl.BlockSpec((tm, d_model), row),
        out_shape=jax.ShapeDtypeStruct((n, d_model), F32),
        compiler_params=pltpu.CompilerParams(
            dimension_semantics=("arbitrary",), vmem_limit_bytes=VMEM_LIMIT_BYTES),
    )(x2d, gaya, gb, on, w_b, w_o, g_mlp, w1, w2, g_fin)


def _lambda_init(layer_idx):
    return 0.8 - 0.6 * math.exp(-0.3 * layer_idx)


def kernel(x, norm_mix_g, w_in, b_gate, conv_w, lambda_q1, lambda_k1, lambda_q2, lambda_k2,
           subln_g, w_a_out, w_b_out, w_o, norm_mlp_g, w_mlp_in, w_mlp_out, norm_final_g):
    batch, seq, d_model = x.shape
    depth = w_in.shape[0]
    x2d = x.reshape(batch * seq, d_model)
    row = lambda a: a.reshape(1, -1)
    for l in range(depth):
        lam_init = _lambda_init(l)
        q, k, vt, gaya, gb, lam = _in_proj(
            x2d, row(norm_mix_g[l]), w_in[l].astype(BF16), row(b_gate[l]), conv_w[l],
            w_a_out[l].astype(BF16), row(lambda_q1[l]), row(lambda_k1[l]),
            row(lambda_q2[l]), row(lambda_k2[l]), seq=seq, lam_init=lam_init)
        on = _diff_attention(q, k, vt, lam, row(subln_g[l]), batch=batch, seq=seq,
                             out_scale=1.0 - lam_init)
        x2d = _out_mlp(
            x2d, gaya, gb, on, w_b_out[l].astype(BF16), w_o[l].astype(BF16),
            row(norm_mlp_g[l]), w_mlp_in[l].astype(BF16), w_mlp_out[l].astype(BF16),
            row(norm_final_g), final_norm=(l == depth - 1))
    return x2d.reshape(batch, seq, d_model)
```

```python
import functools
import math

import jax
import jax.numpy as jnp
import numpy as np
from jax import lax
from jax.experimental import pallas as pl
from jax.experimental.pallas import tpu as pltpu

F32 = jnp.float32
BF16 = jnp.bfloat16

RMS_EPS = 1e-6
CONV_K = 3
DIFF_HEAD_DIM = 64
DIFF_V_DIM = 2 * DIFF_HEAD_DIM

LOG2E = math.log2(math.e)
LANES = 128
SUBLANES = 8
VMEM_LIMIT_BYTES = 56 * 1024 * 1024

TOKEN_TILE = 512
FF_CHUNK = 1024
HEADS_PER_STEP = 4
Q_TILE = 256
KEY_BLOCK = 2 * Q_TILE
POS_SPLIT = 16
SLOPE_PIECES = 3


def _rms_scale(x):
    return lax.rsqrt(jnp.mean(x * x, axis=-1, keepdims=True) + RMS_EPS)


def _dot(a, b):
    return jnp.dot(a, b, preferred_element_type=F32)


def _dot_nt(a, b):
    return lax.dot_general(a, b, (((1,), (1,)), ((), ())), preferred_element_type=F32)


def _resident(shape):
    return pl.BlockSpec(shape, lambda *_: (0,) * len(shape), pipeline_mode=pl.Buffered(1))


def _in_proj_kernel(x0_ref, x_next_ref, g_ref, w_in_ref, b_gate_ref, conv_w_ref, w_a_ref,
                    lq1_ref, lk1_ref, lq2_ref, lk2_ref,
                    q_ref, k_ref, vt_ref, gaya_ref, gb_ref, lam_ref,
                    h_ref, w_vt_ref, xn_ref, *, tiles_per_seq, conv_width, diff_width, d_model,
                    lam_init):
    i = pl.program_id(0)
    tm = xn_ref.shape[0]
    c0 = conv_width
    q0 = 3 * conv_width
    g0 = q0 + 3 * diff_width

    def normed(x_ref):
        x = x_ref[...]
        return ((x * _rms_scale(x)) * g_ref[...]).astype(BF16)

    @pl.when(i == 0)
    def _():
        lam = (jnp.exp(jnp.sum(lq1_ref[...] * lk1_ref[...], axis=-1, keepdims=True))
               - jnp.exp(jnp.sum(lq2_ref[...] * lk2_ref[...], axis=-1, keepdims=True))
               + lam_init)
        lam_ref[...] = jnp.broadcast_to(lam, lam_ref.shape)
        v0 = q0 + 2 * diff_width
        for c in range(0, diff_width, LANES):
            w_vt_ref[c:c + LANES, :] = jnp.transpose(
                w_in_ref[:, v0 + c:v0 + c + LANES].astype(F32)).astype(BF16)
        xn_ref[...] = normed(x0_ref)

    xn = xn_ref[...]

    uc = _dot(xn, w_in_ref[:, 0:q0])
    h = uc[:, c0:2 * c0] * uc[:, 2 * c0:3 * c0]

    @pl.when(i % tiles_per_seq == 0)
    def _():
        h_ref[0:SUBLANES, :] = jnp.zeros((SUBLANES, c0), F32)

    h_ref[SUBLANES:SUBLANES + tm, :] = h
    cw = conv_w_ref[...]
    conv = (cw[0:1, :] * h_ref[SUBLANES - 2:SUBLANES - 2 + tm, :]
            + cw[1:2, :] * h_ref[SUBLANES - 1:SUBLANES - 1 + tm, :]
            + cw[2:3, :] * h)
    h_ref[0:SUBLANES, :] = h[tm - SUBLANES:tm, :]
    gated = (uc[:, 0:c0] * conv).astype(BF16)

    gates = jax.nn.sigmoid(_dot(xn, w_in_ref[:, g0:g0 + 2 * d_model]) + b_gate_ref[...])
    gb_ref[...] = gates[:, d_model:2 * d_model].astype(BF16)

    scale = DIFF_HEAD_DIM ** -0.5 * LOG2E
    q_ref[...] = (_dot(xn, w_in_ref[:, q0:q0 + diff_width]) * scale).astype(BF16)
    k_ref[...] = _dot(xn, w_in_ref[:, q0 + diff_width:q0 + 2 * diff_width]).astype(BF16)
    vt_ref[...] = _dot_nt(w_vt_ref[...], xn).astype(BF16)

    gaya_ref[...] = (gates[:, 0:d_model] * _dot(gated, w_a_ref[...])).astype(BF16)

    xn_ref[...] = normed(x_next_ref)


def _in_proj(x2d, g, w_in, b_gate, conv_w, w_a, lq1, lk1, lq2, lk2, *, seq, lam_init):
    n, d_model = x2d.shape
    conv_width = conv_w.shape[1]
    diff_width = (w_in.shape[1] - 3 * conv_width - 2 * d_model) // 3
    tm = TOKEN_TILE
    assert seq % tm == 0 and n % seq == 0
    steps = n // tm
    row = lambda i: (i, 0)
    kern = functools.partial(
        _in_proj_kernel, tiles_per_seq=seq // tm, conv_width=conv_width,
        diff_width=diff_width, d_model=d_model, lam_init=lam_init)
    return pl.pallas_call(
        kern,
        name="in_proj",
        grid=(steps,),
        in_specs=[
            pl.BlockSpec((tm, d_model), lambda i: (0, 0)),
            pl.BlockSpec((tm, d_model), lambda i: (jnp.minimum(i + 1, steps - 1), 0)),
            _resident(g.shape),
            _resident(w_in.shape),
            _resident(b_gate.shape),
            _resident(conv_w.shape),
            _resident(w_a.shape),
            _resident(lq1.shape), _resident(lk1.shape), _resident(lq2.shape), _resident(lk2.shape),
        ],
        out_specs=[
            pl.BlockSpec((tm, diff_width), row),
            pl.BlockSpec((tm, diff_width), row),
            pl.BlockSpec((diff_width, tm), lambda i: (0, i)),
            pl.BlockSpec((tm, d_model), row),
            pl.BlockSpec((tm, d_model), row),
            pl.BlockSpec((SUBLANES, LANES), lambda i: (0, 0)),
        ],
        out_shape=[
            jax.ShapeDtypeStruct((n, diff_width), BF16),
            jax.ShapeDtypeStruct((n, diff_width), BF16),
            jax.ShapeDtypeStruct((diff_width, n), BF16),
            jax.ShapeDtypeStruct((n, d_model), BF16),
            jax.ShapeDtypeStruct((n, d_model), BF16),
            jax.ShapeDtypeStruct((SUBLANES, LANES), F32),
        ],
        scratch_shapes=[
            pltpu.VMEM((SUBLANES + tm, conv_width), F32),
            pltpu.VMEM((diff_width, d_model), BF16),
            pltpu.VMEM((tm, d_model), BF16),
        ],
        compiler_params=pltpu.CompilerParams(
            dimension_semantics=("arbitrary",), vmem_limit_bytes=VMEM_LIMIT_BYTES),
    )(x2d, x2d, g, w_in, b_gate, conv_w, w_a, lq1, lk1, lq2, lk2)


def _alibi_features(seq, heads):
    pos = np.arange(seq)
    kfeat = np.zeros((seq, LANES), np.float32)
    cfeat = np.zeros((heads, LANES), np.float32)
    as_bf16 = lambda a: a.astype(BF16).astype(np.float32)
    slopes = np.array([2.0 ** (-8.0 * (h + 1) / heads) for h in range(heads)], np.float32)
    rest = (slopes * np.float32(LOG2E)).astype(np.float32)
    for piece in range(SLOPE_PIECES):
        part = as_bf16(rest)
        rest = (rest - part).astype(np.float32)
        kfeat[:, 2 * piece] = POS_SPLIT * (pos // POS_SPLIT)
        kfeat[:, 2 * piece + 1] = pos % POS_SPLIT
        cfeat[:, 2 * piece] = part
        cfeat[:, 2 * piece + 1] = part
    assert seq <= POS_SPLIT * 256 and np.array_equal(as_bf16(kfeat), kfeat)
    return jnp.asarray(kfeat, BF16), jnp.asarray(cfeat, F32)


def _attn_kernel(q_ref, k_ref, kfeat_ref, cfeat_ref, vt_ref, lam_ref, g_ref, o_ref,
                 q2_ref, mask_ref, sx_ref, sy_ref, sx_max_ref, sy_max_ref, m_ref, l_ref, acc_ref,
                 *, out_scale):
    hg = pl.program_id(1)
    i = pl.program_id(2)
    last_tile = pl.num_programs(2) - 1
    t = Q_TILE
    kb = KEY_BLOCK
    d = DIFF_HEAD_DIM
    vd = DIFF_V_DIM
    hps = q2_ref.shape[0]
    sx = (sx_ref, sx_max_ref)
    sy = (sy_ref, sy_max_ref)
    nb = i // 2
    parity = i % 2

    def build_q2(tile):
        rows = pl.ds(pl.multiple_of(tile * t, t), t)
        lane = lax.broadcasted_iota(jnp.int32, (t, vd), 1)
        for hh in range(hps):
            q = q_ref[rows, hh * vd:(hh + 1) * vd]
            zero = jnp.zeros_like(q)
            q2_ref[hh, 0:t, 0:LANES] = jnp.where(lane < d, q, zero)
            q2_ref[hh, t:2 * t, 0:LANES] = jnp.where(lane >= d, q, zero)

    def block_start(b):
        return pl.multiple_of(b * kb, kb)

    def raw_scores(dst, b):
        dst_ref, dst_max_ref = dst
        start = block_start(b)
        feat = kfeat_ref[pl.ds(start, kb), :]
        for hh in range(hps):
            keys = jnp.concatenate([k_ref[pl.ds(start, kb), hh * vd:(hh + 1) * vd], feat], axis=1)
            s = _dot_nt(keys, q2_ref[hh])
            dst_ref[hh] = s
            dst_max_ref[hh] = jnp.max(s, axis=0, keepdims=True)

    def consume(src, b, masked):
        src_ref, src_max_ref = src
        start = block_start(b)
        for hh in range(hps):
            s = src_ref[hh]
            if masked:
                s = s + mask_ref[parity]
                s_max = jnp.max(s, axis=0, keepdims=True)
            else:
                s_max = src_max_ref[hh]
            m_old = m_ref[hh]
            m_new = jnp.maximum(m_old, s_max)
            alpha = jnp.exp2(m_old - m_new)
            p = jnp.exp2(s - m_new)
            m_ref[hh] = m_new
            l_ref[hh] = alpha * l_ref[hh] + jnp.sum(p, axis=0, keepdims=True)
            pv = _dot(vt_ref[hh * vd:(hh + 1) * vd, pl.ds(start, kb)], p.astype(BF16))
            acc_ref[hh] = alpha * acc_ref[hh] + pv

    @pl.when(i == 0)
    def _():
        key = lax.broadcasted_iota(jnp.int32, (kb, 2 * t), 0)
        qry = lax.broadcasted_iota(jnp.int32, (kb, 2 * t), 1)
        qry = jnp.where(qry >= t, qry - t, qry)
        zero = jnp.zeros((kb, 2 * t), F32)
        mask_ref[0] = jnp.where(key <= qry, zero, -jnp.inf)
        mask_ref[1] = jnp.where(key - t <= qry, zero, -jnp.inf)
        for hh in range(hps):
            row = cfeat_ref[pl.ds(hg * hps + hh, 1), :]
            q2_ref[hh, :, LANES:2 * LANES] = jnp.broadcast_to(row, (2 * t, LANES)).astype(BF16)
        build_q2(0)
        raw_scores(sx, 0)

    for hh in range(hps):
        m_ref[hh] = jnp.full((1, 2 * t), -jnp.inf, F32)
        l_ref[hh] = jnp.zeros((1, 2 * t), F32)
        acc_ref[hh] = jnp.zeros((vd, 2 * t), F32)

    def body(pair, carry):
        b = 2 * pair
        raw_scores(sy, b + 1)
        consume(sx, b, False)
        raw_scores(sx, b + 2)
        consume(sy, b + 1, False)
        return carry

    lax.fori_loop(0, nb // 2, body, 0)

    @pl.when(nb % 2 == 1)
    def _():
        raw_scores(sy, nb)
        consume(sx, nb - 1, False)
        consume(sy, nb, True)

    @pl.when(nb % 2 == 0)
    def _():
        consume(sx, nb, True)

    build_q2(jnp.minimum(i + 1, last_tile))
    raw_scores(sx, 0)

    lam = lam_ref[0:1, 0:1]
    for hh in range(hps):
        on = acc_ref[hh] * (1.0 / l_ref[hh])
        o = jnp.transpose(on[:, 0:t] - lam * on[:, t:2 * t])
        o = ((o * _rms_scale(o)) * g_ref[...]) * out_scale
        o_ref[:, hh * vd:(hh + 1) * vd] = o.astype(o_ref.dtype)


def _diff_attention(q, k, vt, lam, subln_g, *, batch, seq, out_scale):
    n, width = q.shape
    heads = width // DIFF_V_DIM
    hps = HEADS_PER_STEP
    t = Q_TILE
    assert seq % KEY_BLOCK == 0 and heads % hps == 0 and DIFF_V_DIM == LANES
    nq = seq // t
    kfeat, cfeat = _alibi_features(seq, heads)
    scores = pltpu.VMEM((hps, KEY_BLOCK, 2 * t), F32)
    per_head_seq = pl.BlockSpec((seq, hps * DIFF_V_DIM), lambda b, h, i: (b, h))
    return pl.pallas_call(
        functools.partial(_attn_kernel, out_scale=out_scale),
        name="diff_attn",
        grid=(batch, heads // hps, nq),
        in_specs=[
            per_head_seq,
            per_head_seq,
            _resident(kfeat.shape),
            _resident(cfeat.shape),
            pl.BlockSpec((hps * DIFF_V_DIM, seq), lambda b, h, i: (h, b)),
            _resident(lam.shape),
            _resident(subln_g.shape),
        ],
        out_specs=pl.BlockSpec((t, hps * DIFF_V_DIM), lambda b, h, i: (b * nq + i, h)),
        out_shape=jax.ShapeDtypeStruct((n, width), BF16),
        scratch_shapes=[
            pltpu.VMEM((hps, 2 * t, 2 * LANES), BF16),
            pltpu.VMEM((2, KEY_BLOCK, 2 * t), F32),
            scores,
            scores,
            pltpu.VMEM((hps, 1, 2 * t), F32),
            pltpu.VMEM((hps, 1, 2 * t), F32),
            pltpu.VMEM((hps, 1, 2 * t), F32),
            pltpu.VMEM((hps, 1, 2 * t), F32),
            pltpu.VMEM((hps, DIFF_V_DIM, 2 * t), F32),
        ],
        compiler_params=pltpu.CompilerParams(
            dimension_semantics=("arbitrary", "arbitrary", "arbitrary"),
            vmem_limit_bytes=VMEM_LIMIT_BYTES),
    )(q, k, kfeat, cfeat, vt, lam, subln_g)


def _out_mlp_kernel(x_ref, gaya_ref, gb_ref, on_ref, w_b_ref, w_o_ref, g_mlp_ref,
                    w1_ref, w2_ref, g_fin_ref, out_ref, *, final_norm):
    y_b = _dot(on_ref[...], w_b_ref[...])
    merged = gaya_ref[...].astype(F32) + gb_ref[...].astype(F32) * y_b
    x1 = x_ref[...] + _dot(merged.astype(BF16), w_o_ref[...])
    xn = ((x1 * _rms_scale(x1)) * g_mlp_ref[...]).astype(BF16)
    d_ff = w1_ref.shape[1]
    acc = x1
    for c in range(0, d_ff, FF_CHUNK):
        hc = jnp.maximum(_dot(xn, w1_ref[:, c:c + FF_CHUNK]), 0.0)
        acc = acc + _dot((hc * hc).astype(BF16), w2_ref[c:c + FF_CHUNK, :])
    if final_norm:
        acc = (acc * _rms_scale(acc)) * g_fin_ref[...]
    out_ref[...] = acc


def _out_mlp(x2d, gaya, gb, on, w_b, w_o, g_mlp, w1, w2, g_fin, *, final_norm):
    n, d_model = x2d.shape
    tm = TOKEN_TILE
    assert n % tm == 0 and w1.shape[1] % FF_CHUNK == 0
    row = lambda i: (i, 0)
    return pl.pallas_call(
        functools.partial(_out_mlp_kernel, final_norm=final_norm),
        name="out_mlp",
        grid=(n // tm,),
        in_specs=[
            pl.BlockSpec((tm, d_model), row),
            pl.BlockSpec((tm, d_model), row),
            pl.BlockSpec((tm, d_model), row),
            pl.BlockSpec((tm, on.shape[1]), row),
            _resident(w_b.shape),
            _resident(w_o.shape),
            _resident(g_mlp.shape),
            _resident(w1.shape),
            _resident(w2.shape),
            _resident(g_fin.shape),
        ],
        out_specs=pl.BlockSpec((tm, d_model), row),
        out_shape=jax.ShapeDtypeStruct((n, d_model), F32),
        compiler_params=pltpu.CompilerParams(
            dimension_semantics=("arbitrary",), vmem_limit_bytes=VMEM_LIMIT_BYTES),
    )(x2d, gaya, gb, on, w_b, w_o, g_mlp, w1, w2, g_fin)


def _lambda_init(layer_idx):
    return 0.8 - 0.6 * math.exp(-0.3 * layer_idx)


def kernel(x, norm_mix_g, w_in, b_gate, conv_w, lambda_q1, lambda_k1, lambda_q2, lambda_k2,
           subln_g, w_a_out, w_b_out, w_o, norm_mlp_g, w_mlp_in, w_mlp_out, norm_final_g):
    batch, seq, d_model = x.shape
    depth = w_in.shape[0]
    x2d = x.reshape(batch * seq, d_model)
    row = lambda a: a.reshape(1, -1)
    for l in range(depth):
        lam_init = _lambda_init(l)
        q, k, vt, gaya, gb, lam = _in_proj(
            x2d, row(norm_mix_g[l]), w_in[l].astype(BF16), row(b_gate[l]), conv_w[l],
            w_a_out[l].astype(BF16), row(lambda_q1[l]), row(lambda_k1[l]),
            row(lambda_q2[l]), row(lambda_k2[l]), seq=seq, lam_init=lam_init)
        on = _diff_attention(q, k, vt, lam, row(subln_g[l]), batch=batch, seq=seq,
                             out_scale=1.0 - lam_init)
        x2d = _out_mlp(
            x2d, gaya, gb, on, w_b_out[l].astype(BF16), w_o[l].astype(BF16),
            row(norm_mlp_g[l]), w_mlp_in[l].astype(BF16), w_mlp_out[l].astype(BF16),
            row(norm_final_g), final_norm=(l == depth - 1))
    return x2d.reshape(batch, seq, d_model)
```

```python
import functools
import math

import jax
import jax.numpy as jnp
import numpy as np
from jax import lax
from jax.experimental import pallas as pl
from jax.experimental.pallas import tpu as pltpu

F32 = jnp.float32
BF16 = jnp.bfloat16

RMS_EPS = 1e-6
CONV_K = 3
DIFF_HEAD_DIM = 64
DIFF_V_DIM = 2 * DIFF_HEAD_DIM

LOG2E = math.log2(math.e)
LANES = 128
SUBLANES = 8
VMEM_LIMIT_BYTES = 56 * 1024 * 1024

TOKEN_TILE = 512
FF_CHUNK = 1024
HEADS_PER_STEP = 4
Q_TILE = 256
KEY_BLOCK = 2 * Q_TILE
POS_SPLIT = 16
SLOPE_PIECES = 3


def _rms_scale(x):
    return lax.rsqrt(jnp.mean(x * x, axis=-1, keepdims=True) + RMS_EPS)


def _dot(a, b):
    return jnp.dot(a, b, preferred_element_type=F32)


def _dot_nt(a, b):
    return lax.dot_general(a, b, (((1,), (1,)), ((), ())), preferred_element_type=F32)


def _resident(shape):
    return pl.BlockSpec(shape, lambda *_: (0,) * len(shape), pipeline_mode=pl.Buffered(1))


def _in_proj_kernel(x0_ref, x_next_ref, g_ref, w_in_ref, b_gate_ref, conv_w_ref, w_a_ref,
                    lq1_ref, lk1_ref, lq2_ref, lk2_ref,
                    q_ref, k_ref, vt_ref, gaya_ref, gb_ref, lam_ref,
                    h_ref, w_vt_ref, xn_ref, *, tiles_per_seq, conv_width, diff_width, d_model,
                    lam_init):
    i = pl.program_id(0)
    tm = xn_ref.shape[0]
    c0 = conv_width
    q0 = 3 * conv_width
    g0 = q0 + 3 * diff_width

    def normed(x_ref):
        x = x_ref[...]
        return ((x * _rms_scale(x)) * g_ref[...]).astype(BF16)

    @pl.when(i == 0)
    def _():
        lam = (jnp.exp(jnp.sum(lq1_ref[...] * lk1_ref[...], axis=-1, keepdims=True))
               - jnp.exp(jnp.sum(lq2_ref[...] * lk2_ref[...], axis=-1, keepdims=True))
               + lam_init)
        lam_ref[...] = jnp.broadcast_to(lam, lam_ref.shape)
        v0 = q0 + 2 * diff_width
        for c in range(0, diff_width, LANES):
            w_vt_ref[c:c + LANES, :] = jnp.transpose(
                w_in_ref[:, v0 + c:v0 + c + LANES].astype(F32)).astype(BF16)
        xn_ref[...] = normed(x0_ref)

    xn = xn_ref[...]

    uc = _dot(xn, w_in_ref[:, 0:q0])
    h = uc[:, c0:2 * c0] * uc[:, 2 * c0:3 * c0]

    @pl.when(i % tiles_per_seq == 0)
    def _():
        h_ref[0:SUBLANES, :] = jnp.zeros((SUBLANES, c0), F32)

    h_ref[SUBLANES:SUBLANES + tm, :] = h
    cw = conv_w_ref[...]
    conv = (cw[0:1, :] * h_ref[SUBLANES - 2:SUBLANES - 2 + tm, :]
            + cw[1:2, :] * h_ref[SUBLANES - 1:SUBLANES - 1 + tm, :]
            + cw[2:3, :] * h)
    h_ref[0:SUBLANES, :] = h[tm - SUBLANES:tm, :]
    gated = (uc[:, 0:c0] * conv).astype(BF16)

    gates = jax.nn.sigmoid(_dot(xn, w_in_ref[:, g0:g0 + 2 * d_model]) + b_gate_ref[...])
    gb_ref[...] = gates[:, d_model:2 * d_model].astype(BF16)

    scale = DIFF_HEAD_DIM ** -0.5 * LOG2E
    q_ref[...] = (_dot(xn, w_in_ref[:, q0:q0 + diff_width]) * scale).astype(BF16)
    k_ref[...] = _dot(xn, w_in_ref[:, q0 + diff_width:q0 + 2 * diff_width]).astype(BF16)
    vt_ref[...] = _dot_nt(w_vt_ref[...], xn).astype(BF16)

    gaya_ref[...] = (gates[:, 0:d_model] * _dot(gated, w_a_ref[...])).astype(BF16)

    xn_ref[...] = normed(x_next_ref)


def _in_proj(x2d, g, w_in, b_gate, conv_w, w_a, lq1, lk1, lq2, lk2, *, seq, lam_init):
    n, d_model = x2d.shape
    conv_width = conv_w.shape[1]
    diff_width = (w_in.shape[1] - 3 * conv_width - 2 * d_model) // 3
    tm = TOKEN_TILE
    assert seq % tm == 0 and n % seq == 0
    steps = n // tm
    row = lambda i: (i, 0)
    kern = functools.partial(
        _in_proj_kernel, tiles_per_seq=seq // tm, conv_width=conv_width,
        diff_width=diff_width, d_model=d_model, lam_init=lam_init)
    return pl.pallas_call(
        kern,
        name="in_proj",
        grid=(steps,),
        in_specs=[
            pl.BlockSpec((tm, d_model), lambda i: (0, 0)),
            pl.BlockSpec((tm, d_model), lambda i: (jnp.minimum(i + 1, steps - 1), 0)),
            _resident(g.shape),
            _resident(w_in.shape),
            _resident(b_gate.shape),
            _resident(conv_w.shape),
            _resident(w_a.shape),
            _resident(lq1.shape), _resident(lk1.shape), _resident(lq2.shape), _resident(lk2.shape),
        ],
        out_specs=[
            pl.BlockSpec((tm, diff_width), row),
            pl.BlockSpec((tm, diff_width), row),
            pl.BlockSpec((diff_width, tm), lambda i: (0, i)),
            pl.BlockSpec((tm, d_model), row),
            pl.BlockSpec((tm, d_model), row),
            pl.BlockSpec((SUBLANES, LANES), lambda i: (0, 0)),
        ],
        out_shape=[
            jax.ShapeDtypeStruct((n, diff_width), BF16),
            jax.ShapeDtypeStruct((n, diff_width), BF16),
            jax.ShapeDtypeStruct((diff_width, n), BF16),
            jax.ShapeDtypeStruct((n, d_model), BF16),
            jax.ShapeDtypeStruct((n, d_model), BF16),
            jax.ShapeDtypeStruct((SUBLANES, LANES), F32),
        ],
        scratch_shapes=[
            pltpu.VMEM((SUBLANES + tm, conv_width), F32),
            pltpu.VMEM((diff_width, d_model), BF16),
            pltpu.VMEM((tm, d_model), BF16),
        ],
        compiler_params=pltpu.CompilerParams(
            dimension_semantics=("arbitrary",), vmem_limit_bytes=VMEM_LIMIT_BYTES),
    )(x2d, x2d, g, w_in, b_gate, conv_w, w_a, lq1, lk1, lq2, lk2)


def _alibi_features(seq, heads):
    pos = np.arange(seq)
    kfeat = np.zeros((seq, LANES), np.float32)
    cfeat = np.zeros((heads, LANES), np.float32)
    as_bf16 = lambda a: a.astype(BF16).astype(np.float32)
    slopes = np.array([2.0 ** (-8.0 * (h + 1) / heads) for h in range(heads)], np.float32)
    rest = (slopes * np.float32(LOG2E)).astype(np.float32)
    for piece in range(SLOPE_PIECES):
        part = as_bf16(rest)
        rest = (rest - part).astype(np.float32)
        kfeat[:, 2 * piece] = POS_SPLIT * (pos // POS_SPLIT)
        kfeat[:, 2 * piece + 1] = pos % POS_SPLIT
        cfeat[:, 2 * piece] = part
        cfeat[:, 2 * piece + 1] = part
    assert seq <= POS_SPLIT * 256 and np.array_equal(as_bf16(kfeat), kfeat)
    return jnp.asarray(kfeat, BF16), jnp.asarray(cfeat, F32)


def _attn_kernel(q_ref, k_ref, kfeat_ref, cfeat_ref, vt_ref, lam_ref, g_ref, o_ref,
                 q2_ref, mask_ref, sx_ref, sy_ref, sx_max_ref, sy_max_ref, m_ref, l_ref, acc_ref,
                 *, out_scale):
    hg = pl.program_id(1)
    i = pl.program_id(2)
    last_tile = pl.num_programs(2) - 1
    t = Q_TILE
    kb = KEY_BLOCK
    d = DIFF_HEAD_DIM
    vd = DIFF_V_DIM
    hps = q2_ref.shape[0]
    sx = (sx_ref, sx_max_ref)
    sy = (sy_ref, sy_max_ref)
    nb = i // 2
    parity = i % 2

    def build_q2(tile):
        rows = pl.ds(pl.multiple_of(tile * t, t), t)
        lane = lax.broadcasted_iota(jnp.int32, (t, vd), 1)
        for hh in range(hps):
            q = q_ref[rows, hh * vd:(hh + 1) * vd]
            zero = jnp.zeros_like(q)
            q2_ref[hh, 0:t, 0:LANES] = jnp.where(lane < d, q, zero)
            q2_ref[hh, t:2 * t, 0:LANES] = jnp.where(lane >= d, q, zero)

    def block_start(b):
        return pl.multiple_of(b * kb, kb)

    def raw_scores(dst, b):
        dst_ref, dst_max_ref = dst
        start = block_start(b)
        feat = kfeat_ref[pl.ds(start, kb), :]
        for hh in range(hps):
            keys = jnp.concatenate([k_ref[pl.ds(start, kb), hh * vd:(hh + 1) * vd], feat], axis=1)
            s = _dot_nt(keys, q2_ref[hh])
            dst_ref[hh] = s
            dst_max_ref[hh] = jnp.max(s, axis=0, keepdims=True)

    def consume(src, b, mask=None, rows=kb):
        src_ref, src_max_ref = src
        start = block_start(b)
        for hh in range(hps):
            s = src_ref[hh, 0:rows, :]
            if mask is None:
                s_max = src_max_ref[hh]
            else:
                s = s + mask_ref[mask, 0:rows, :]
                s_max = jnp.max(s, axis=0, keepdims=True)
            m_old = m_ref[hh]
            m_new = jnp.maximum(m_old, s_max)
            alpha = jnp.exp2(m_old - m_new)
            p = jnp.exp2(s - m_new)
            m_ref[hh] = m_new
            l_ref[hh] = alpha * l_ref[hh] + jnp.sum(p, axis=0, keepdims=True)
            pv = _dot(vt_ref[hh * vd:(hh + 1) * vd, pl.ds(start, rows)], p.astype(BF16))
            acc_ref[hh] = alpha * acc_ref[hh] + pv

    def consume_masked(src, b, diag_first):
        if diag_first:
            consume(src, b, mask=0, rows=t)
        else:
            consume(src, b, mask=1)

    @pl.when(i == 0)
    def _():
        key = lax.broadcasted_iota(jnp.int32, (kb, 2 * t), 0)
        qry = lax.broadcasted_iota(jnp.int32, (kb, 2 * t), 1)
        qry = jnp.where(qry >= t, qry - t, qry)
        zero = jnp.zeros((kb, 2 * t), F32)
        mask_ref[0] = jnp.where(key <= qry, zero, -jnp.inf)
        mask_ref[1] = jnp.where(key - t <= qry, zero, -jnp.inf)
        for hh in range(hps):
            row = cfeat_ref[pl.ds(hg * hps + hh, 1), :]
            q2_ref[hh, :, LANES:2 * LANES] = jnp.broadcast_to(row, (2 * t, LANES)).astype(BF16)
        build_q2(0)
        raw_scores(sx, 0)

    for hh in range(hps):
        m_ref[hh] = jnp.full((1, 2 * t), -jnp.inf, F32)
        l_ref[hh] = jnp.zeros((1, 2 * t), F32)
        acc_ref[hh] = jnp.zeros((vd, 2 * t), F32)

    def body(pair, carry):
        b = 2 * pair
        raw_scores(sy, b + 1)
        consume(sx, b)
        raw_scores(sx, b + 2)
        consume(sy, b + 1)
        return carry

    lax.fori_loop(0, nb // 2, body, 0)

    for diag_first in (True, False):
        @pl.when((nb % 2 == 1) & (parity == (0 if diag_first else 1)))
        def _():
            raw_scores(sy, nb)
            consume(sx, nb - 1)
            consume_masked(sy, nb, diag_first)

        @pl.when((nb % 2 == 0) & (parity == (0 if diag_first else 1)))
        def _():
            consume_masked(sx, nb, diag_first)

    build_q2(jnp.minimum(i + 1, last_tile))
    raw_scores(sx, 0)

    lam = lam_ref[0:1, 0:1]
    for hh in range(hps):
        on = acc_ref[hh] * (1.0 / l_ref[hh])
        o = jnp.transpose(on[:, 0:t] - lam * on[:, t:2 * t])
        o = ((o * _rms_scale(o)) * g_ref[...]) * out_scale
        o_ref[:, hh * vd:(hh + 1) * vd] = o.astype(o_ref.dtype)


def _diff_attention(q, k, vt, lam, subln_g, *, batch, seq, out_scale):
    n, width = q.shape
    heads = width // DIFF_V_DIM
    hps = HEADS_PER_STEP
    t = Q_TILE
    assert seq % KEY_BLOCK == 0 and heads % hps == 0 and DIFF_V_DIM == LANES
    nq = seq // t
    kfeat, cfeat = _alibi_features(seq, heads)
    scores = pltpu.VMEM((hps, KEY_BLOCK, 2 * t), F32)
    per_head_seq = pl.BlockSpec((seq, hps * DIFF_V_DIM), lambda b, h, i: (b, h))
    return pl.pallas_call(
        functools.partial(_attn_kernel, out_scale=out_scale),
        name="diff_attn",
        grid=(batch, heads // hps, nq),
        in_specs=[
            per_head_seq,
            per_head_seq,
            _resident(kfeat.shape),
            _resident(cfeat.shape),
            pl.BlockSpec((hps * DIFF_V_DIM, seq), lambda b, h, i: (h, b)),
            _resident(lam.shape),
            _resident(subln_g.shape),
        ],
        out_specs=pl.BlockSpec((t, hps * DIFF_V_DIM), lambda b, h, i: (b * nq + i, h)),
        out_shape=jax.ShapeDtypeStruct((n, width), BF16),
        scratch_shapes=[
            pltpu.VMEM((hps, 2 * t, 2 * LANES), BF16),
            pltpu.VMEM((2, KEY_BLOCK, 2 * t), F32),
            scores,
            scores,
            pltpu.VMEM((hps, 1, 2 * t), F32),
            pltpu.VMEM((hps, 1, 2 * t), F32),
            pltpu.VMEM((hps, 1, 2 * t), F32),
            pltpu.VMEM((hps, 1, 2 * t), F32),
            pltpu.VMEM((hps, DIFF_V_DIM, 2 * t), F32),
        ],
        compiler_params=pltpu.CompilerParams(
            dimension_semantics=("arbitrary", "arbitrary", "arbitrary"),
            vmem_limit_bytes=VMEM_LIMIT_BYTES),
    )(q, k, kfeat, cfeat, vt, lam, subln_g)


def _out_mlp_kernel(x_ref, gaya_ref, gb_ref, on_ref, w_b_ref, w_o_ref, g_mlp_ref,
                    w1_ref, w2_ref, g_fin_ref, out_ref, *, final_norm):
    y_b = _dot(on_ref[...], w_b_ref[...])
    merged = gaya_ref[...].astype(F32) + gb_ref[...].astype(F32) * y_b
    x1 = x_ref[...] + _dot(merged.astype(BF16), w_o_ref[...])
    xn = ((x1 * _rms_scale(x1)) * g_mlp_ref[...]).astype(BF16)
    d_ff = w1_ref.shape[1]
    acc = x1
    for c in range(0, d_ff, FF_CHUNK):
        hc = jnp.maximum(_dot(xn, w1_ref[:, c:c + FF_CHUNK]), 0.0)
        acc = acc + _dot((hc * hc).astype(BF16), w2_ref[c:c + FF_CHUNK, :])
    if final_norm:
        acc = (acc * _rms_scale(acc)) * g_fin_ref[...]
    out_ref[...] = acc


def _out_mlp(x2d, gaya, gb, on, w_b, w_o, g_mlp, w1, w2, g_fin, *, final_norm):
    n, d_model = x2d.shape
    tm = TOKEN_TILE
    assert n % tm == 0 and w1.shape[1] % FF_CHUNK == 0
    row = lambda i: (i, 0)
    return pl.pallas_call(
        functools.partial(_out_mlp_kernel, final_norm=final_norm),
        name="out_mlp",
        grid=(n // tm,),
        in_specs=[
            pl.BlockSpec((tm, d_model), row),
            pl.BlockSpec((tm, d_model), row),
            pl.BlockSpec((tm, d_model), row),
            pl.BlockSpec((tm, on.shape[1]), row),
            _resident(w_b.shape),
            _resident(w_o.shape),
            _resident(g_mlp.shape),
            _resident(w1.shape),
            _resident(w2.shape),
            _resident(g_fin.shape),
        ],
        out_specs=pl.BlockSpec((tm, d_model), row),
        out_shape=jax.ShapeDtypeStruct((n, d_model), F32),
        compiler_params=pltpu.CompilerParams(
            dimension_semantics=("arbitrary",), vmem_limit_bytes=VMEM_LIMIT_BYTES),
    )(x2d, gaya, gb, on, w_b, w_o, g_mlp, w1, w2, g_fin)


def _lambda_init(layer_idx):
    return 0.8 - 0.6 * math.exp(-0.3 * layer_idx)


def kernel(x, norm_mix_g, w_in, b_gate, conv_w, lambda_q1, lambda_k1, lambda_q2, lambda_k2,
           subln_g, w_a_out, w_b_out, w_o, norm_mlp_g, w_mlp_in, w_mlp_out, norm_final_g):
    batch, seq, d_model = x.shape
    depth = w_in.shape[0]
    x2d = x.reshape(batch * seq, d_model)
    row = lambda a: a.reshape(1, -1)
    for l in range(depth):
        lam_init = _lambda_init(l)
        q, k, vt, gaya, gb, lam = _in_proj(
            x2d, row(norm_mix_g[l]), w_in[l].astype(BF16), row(b_gate[l]), conv_w[l],
            w_a_out[l].astype(BF16), row(lambda_q1[l]), row(lambda_k1[l]),
            row(lambda_q2[l]), row(lambda_k2[l]), seq=seq, lam_init=lam_init)
        on = _diff_attention(q, k, vt, lam, row(subln_g[l]), batch=batch, seq=seq,
                             out_scale=1.0 - lam_init)
        x2d = _out_mlp(
            x2d, gaya, gb, on, w_b_out[l].astype(BF16), w_o[l].astype(BF16),
            row(norm_mlp_g[l]), w_mlp_in[l].astype(BF16), w_mlp_out[l].astype(BF16),
            row(norm_final_g), final_norm=(l == depth - 1))
    return x2d.reshape(batch, seq, d_model)
```

```python
import functools
import math

import jax
import jax.numpy as jnp
import numpy as np
from jax import lax
from jax.experimental import pallas as pl
from jax.experimental.pallas import tpu as pltpu

F32 = jnp.float32
BF16 = jnp.bfloat16

RMS_EPS = 1e-6
CONV_K = 3
DIFF_HEAD_DIM = 64
DIFF_V_DIM = 2 * DIFF_HEAD_DIM

LOG2E = math.log2(math.e)
LANES = 128
SUM_ROWS = 16
VT_ROWS = DIFF_V_DIM + SUM_ROWS
SUBLANES = 8
VMEM_LIMIT_BYTES = 56 * 1024 * 1024

TOKEN_TILE = 512
FF_CHUNK = 1024
HEADS_PER_STEP = 4
Q_TILE = 256
KEY_BLOCK = 2 * Q_TILE
POS_SPLIT = 16
SLOPE_PIECES = 3


def _rms_scale(x):
    return lax.rsqrt(jnp.mean(x * x, axis=-1, keepdims=True) + RMS_EPS)


def _dot(a, b):
    return jnp.dot(a, b, preferred_element_type=F32)


def _dot_nt(a, b):
    return lax.dot_general(a, b, (((1,), (1,)), ((), ())), preferred_element_type=F32)


def _resident(shape):
    return pl.BlockSpec(shape, lambda *_: (0,) * len(shape), pipeline_mode=pl.Buffered(1))


def _in_proj_kernel(x0_ref, x_next_ref, g_ref, w_in_ref, b_gate_ref, conv_w_ref, w_a_ref,
                    lq1_ref, lk1_ref, lq2_ref, lk2_ref,
                    q_ref, k_ref, vt_ref, gaya_ref, gb_ref, lam_ref,
                    h_ref, w_vt_ref, xn_ref, *, tiles_per_seq, conv_width, diff_width, d_model,
                    lam_init):
    i = pl.program_id(0)
    tm = xn_ref.shape[0]
    c0 = conv_width
    q0 = 3 * conv_width
    g0 = q0 + 3 * diff_width

    def normed(x_ref):
        x = x_ref[...]
        return ((x * _rms_scale(x)) * g_ref[...]).astype(BF16)

    @pl.when(i == 0)
    def _():
        lam = (jnp.exp(jnp.sum(lq1_ref[...] * lk1_ref[...], axis=-1, keepdims=True))
               - jnp.exp(jnp.sum(lq2_ref[...] * lk2_ref[...], axis=-1, keepdims=True))
               + lam_init)
        lam_ref[...] = jnp.broadcast_to(lam, lam_ref.shape)
        v0 = q0 + 2 * diff_width
        for c in range(0, diff_width, LANES):
            w_vt_ref[c:c + LANES, :] = jnp.transpose(
                w_in_ref[:, v0 + c:v0 + c + LANES].astype(F32)).astype(BF16)
        xn_ref[...] = normed(x0_ref)

    xn = xn_ref[...]

    uc = _dot(xn, w_in_ref[:, 0:q0])
    h = uc[:, c0:2 * c0] * uc[:, 2 * c0:3 * c0]

    @pl.when(i % tiles_per_seq == 0)
    def _():
        h_ref[0:SUBLANES, :] = jnp.zeros((SUBLANES, c0), F32)

    h_ref[SUBLANES:SUBLANES + tm, :] = h
    cw = conv_w_ref[...]
    conv = (cw[0:1, :] * h_ref[SUBLANES - 2:SUBLANES - 2 + tm, :]
            + cw[1:2, :] * h_ref[SUBLANES - 1:SUBLANES - 1 + tm, :]
            + cw[2:3, :] * h)
    h_ref[0:SUBLANES, :] = h[tm - SUBLANES:tm, :]
    gated = (uc[:, 0:c0] * conv).astype(BF16)

    gates = jax.nn.sigmoid(_dot(xn, w_in_ref[:, g0:g0 + 2 * d_model]) + b_gate_ref[...])
    gb_ref[...] = gates[:, d_model:2 * d_model].astype(BF16)

    scale = DIFF_HEAD_DIM ** -0.5 * LOG2E
    q_ref[...] = (_dot(xn, w_in_ref[:, q0:q0 + diff_width]) * scale).astype(BF16)
    k_ref[...] = _dot(xn, w_in_ref[:, q0 + diff_width:q0 + 2 * diff_width]).astype(BF16)
    vt = _dot_nt(w_vt_ref[...], xn).astype(BF16)
    for hh in range(diff_width // DIFF_V_DIM):
        r0 = hh * VT_ROWS
        vt_ref[r0:r0 + DIFF_V_DIM, :] = vt[hh * DIFF_V_DIM:(hh + 1) * DIFF_V_DIM, :]
        vt_ref[r0 + DIFF_V_DIM:r0 + VT_ROWS, :] = jnp.ones((SUM_ROWS, tm), BF16)

    gaya_ref[...] = (gates[:, 0:d_model] * _dot(gated, w_a_ref[...])).astype(BF16)

    xn_ref[...] = normed(x_next_ref)


def _in_proj(x2d, g, w_in, b_gate, conv_w, w_a, lq1, lk1, lq2, lk2, *, seq, lam_init):
    n, d_model = x2d.shape
    conv_width = conv_w.shape[1]
    diff_width = (w_in.shape[1] - 3 * conv_width - 2 * d_model) // 3
    tm = TOKEN_TILE
    assert seq % tm == 0 and n % seq == 0
    steps = n // tm
    vt_rows = (diff_width // DIFF_V_DIM) * VT_ROWS
    row = lambda i: (i, 0)
    kern = functools.partial(
        _in_proj_kernel, tiles_per_seq=seq // tm, conv_width=conv_width,
        diff_width=diff_width, d_model=d_model, lam_init=lam_init)
    return pl.pallas_call(
        kern,
        name="in_proj",
        grid=(steps,),
        in_specs=[
            pl.BlockSpec((tm, d_model), lambda i: (0, 0)),
            pl.BlockSpec((tm, d_model), lambda i: (jnp.minimum(i + 1, steps - 1), 0)),
            _resident(g.shape),
            _resident(w_in.shape),
            _resident(b_gate.shape),
            _resident(conv_w.shape),
            _resident(w_a.shape),
            _resident(lq1.shape), _resident(lk1.shape), _resident(lq2.shape), _resident(lk2.shape),
        ],
        out_specs=[
            pl.BlockSpec((tm, diff_width), row),
            pl.BlockSpec((tm, diff_width), row),
            pl.BlockSpec((vt_rows, tm), lambda i: (0, i)),
            pl.BlockSpec((tm, d_model), row),
            pl.BlockSpec((tm, d_model), row),
            pl.BlockSpec((SUBLANES, LANES), lambda i: (0, 0)),
        ],
        out_shape=[
            jax.ShapeDtypeStruct((n, diff_width), BF16),
            jax.ShapeDtypeStruct((n, diff_width), BF16),
            jax.ShapeDtypeStruct((vt_rows, n), BF16),
            jax.ShapeDtypeStruct((n, d_model), BF16),
            jax.ShapeDtypeStruct((n, d_model), BF16),
            jax.ShapeDtypeStruct((SUBLANES, LANES), F32),
        ],
        scratch_shapes=[
            pltpu.VMEM((SUBLANES + tm, conv_width), F32),
            pltpu.VMEM((diff_width, d_model), BF16),
            pltpu.VMEM((tm, d_model), BF16),
        ],
        compiler_params=pltpu.CompilerParams(
            dimension_semantics=("arbitrary",), vmem_limit_bytes=VMEM_LIMIT_BYTES),
    )(x2d, x2d, g, w_in, b_gate, conv_w, w_a, lq1, lk1, lq2, lk2)


def _alibi_features(seq, heads):
    pos = np.arange(seq)
    kfeat = np.zeros((seq, LANES), np.float32)
    cfeat = np.zeros((heads, LANES), np.float32)
    as_bf16 = lambda a: a.astype(BF16).astype(np.float32)
    slopes = np.array([2.0 ** (-8.0 * (h + 1) / heads) for h in range(heads)], np.float32)
    rest = (slopes * np.float32(LOG2E)).astype(np.float32)
    for piece in range(SLOPE_PIECES):
        part = as_bf16(rest)
        rest = (rest - part).astype(np.float32)
        kfeat[:, 2 * piece] = POS_SPLIT * (pos // POS_SPLIT)
        kfeat[:, 2 * piece + 1] = pos % POS_SPLIT
        cfeat[:, 2 * piece] = part
        cfeat[:, 2 * piece + 1] = part
    assert seq <= POS_SPLIT * 256 and np.array_equal(as_bf16(kfeat), kfeat)
    return jnp.asarray(kfeat, BF16), jnp.asarray(cfeat, F32)


def _attn_kernel(q_ref, k_ref, kfeat_ref, cfeat_ref, vt_ref, lam_ref, g_ref, o_ref,
                 q2_ref, mask_ref, sx_ref, sy_ref, sx_max_ref, sy_max_ref, m_ref, acc_ref,
                 *, out_scale):
    hg = pl.program_id(1)
    i = pl.program_id(2)
    last_tile = pl.num_programs(2) - 1
    t = Q_TILE
    kb = KEY_BLOCK
    d = DIFF_HEAD_DIM
    vd = DIFF_V_DIM
    hps = q2_ref.shape[0]
    sx = (sx_ref, sx_max_ref)
    sy = (sy_ref, sy_max_ref)
    nb = i // 2
    parity = i % 2

    def build_q2(tile):
        rows = pl.ds(pl.multiple_of(tile * t, t), t)
        lane = lax.broadcasted_iota(jnp.int32, (t, vd), 1)
        for hh in range(hps):
            q = q_ref[rows, hh * vd:(hh + 1) * vd]
            zero = jnp.zeros_like(q)
            q2_ref[hh, 0:t, 0:LANES] = jnp.where(lane < d, q, zero)
            q2_ref[hh, t:2 * t, 0:LANES] = jnp.where(lane >= d, q, zero)

    def block_start(b):
        return pl.multiple_of(b * kb, kb)

    def raw_scores(dst, b):
        dst_ref, dst_max_ref = dst
        start = block_start(b)
        feat = kfeat_ref[pl.ds(start, kb), :]
        for hh in range(hps):
            keys = jnp.concatenate([k_ref[pl.ds(start, kb), hh * vd:(hh + 1) * vd], feat], axis=1)
            s = _dot_nt(keys, q2_ref[hh])
            dst_ref[hh] = s
            dst_max_ref[hh] = jnp.max(s, axis=0, keepdims=True)

    def consume(src, b, mask=None, rows=kb):
        src_ref, src_max_ref = src
        start = block_start(b)
        for hh in range(hps):
            s = src_ref[hh, 0:rows, :]
            if mask is None:
                s_max = src_max_ref[hh]
            else:
                s = s + mask_ref[mask, 0:rows, :]
                s_max = jnp.max(s, axis=0, keepdims=True)
            m_old = m_ref[hh]
            m_new = jnp.maximum(m_old, s_max)
            alpha = jnp.exp2(m_old - m_new)
            p = jnp.exp2(s - m_new)
            m_ref[hh] = m_new
            pv = _dot(vt_ref[hh * VT_ROWS:(hh + 1) * VT_ROWS, pl.ds(start, rows)], p.astype(BF16))
            acc_ref[hh] = alpha * acc_ref[hh] + pv

    def consume_masked(src, b, diag_first):
        if diag_first:
            consume(src, b, mask=0, rows=t)
        else:
            consume(src, b, mask=1)

    @pl.when(i == 0)
    def _():
        key = lax.broadcasted_iota(jnp.int32, (kb, 2 * t), 0)
        qry = lax.broadcasted_iota(jnp.int32, (kb, 2 * t), 1)
        qry = jnp.where(qry >= t, qry - t, qry)
        zero = jnp.zeros((kb, 2 * t), F32)
        mask_ref[0] = jnp.where(key <= qry, zero, -jnp.inf)
        mask_ref[1] = jnp.where(key - t <= qry, zero, -jnp.inf)
        for hh in range(hps):
            row = cfeat_ref[pl.ds(hg * hps + hh, 1), :]
            q2_ref[hh, :, LANES:2 * LANES] = jnp.broadcast_to(row, (2 * t, LANES)).astype(BF16)
        build_q2(0)
        raw_scores(sx, 0)

    for hh in range(hps):
        m_ref[hh] = jnp.full((1, 2 * t), -jnp.inf, F32)
        acc_ref[hh] = jnp.zeros((VT_ROWS, 2 * t), F32)

    def body(pair, carry):
        b = 2 * pair
        raw_scores(sy, b + 1)
        consume(sx, b)
        raw_scores(sx, b + 2)
        consume(sy, b + 1)
        return carry

    lax.fori_loop(0, nb // 2, body, 0)

    for diag_first in (True, False):
        @pl.when((nb % 2 == 1) & (parity == (0 if diag_first else 1)))
        def _():
            raw_scores(sy, nb)
            consume(sx, nb - 1)
            consume_masked(sy, nb, diag_first)

        @pl.when((nb % 2 == 0) & (parity == (0 if diag_first else 1)))
        def _():
            consume_masked(sx, nb, diag_first)

    build_q2(jnp.minimum(i + 1, last_tile))
    raw_scores(sx, 0)

    lam = lam_ref[0:1, 0:1]
    for hh in range(hps):
        on = acc_ref[hh, 0:vd, :] * (1.0 / acc_ref[hh, vd:vd + 1, :])
        o = jnp.transpose(on[:, 0:t] - lam * on[:, t:2 * t])
        o = ((o * _rms_scale(o)) * g_ref[...]) * out_scale
        o_ref[:, hh * vd:(hh + 1) * vd] = o.astype(o_ref.dtype)


def _diff_attention(q, k, vt, lam, subln_g, *, batch, seq, out_scale):
    n, width = q.shape
    heads = width // DIFF_V_DIM
    hps = HEADS_PER_STEP
    t = Q_TILE
    assert seq % KEY_BLOCK == 0 and heads % hps == 0 and DIFF_V_DIM == LANES
    nq = seq // t
    kfeat, cfeat = _alibi_features(seq, heads)
    scores = pltpu.VMEM((hps, KEY_BLOCK, 2 * t), F32)
    per_head_seq = pl.BlockSpec((seq, hps * DIFF_V_DIM), lambda b, h, i: (b, h))
    return pl.pallas_call(
        functools.partial(_attn_kernel, out_scale=out_scale),
        name="diff_attn",
        grid=(batch, heads // hps, nq),
        in_specs=[
            per_head_seq,
            per_head_seq,
            _resident(kfeat.shape),
            _resident(cfeat.shape),
            pl.BlockSpec((hps * VT_ROWS, seq), lambda b, h, i: (h, b)),
            _resident(lam.shape),
            _resident(subln_g.shape),
        ],
        out_specs=pl.BlockSpec((t, hps * DIFF_V_DIM), lambda b, h, i: (b * nq + i, h)),
        out_shape=jax.ShapeDtypeStruct((n, width), BF16),
        scratch_shapes=[
            pltpu.VMEM((hps, 2 * t, 2 * LANES), BF16),
            pltpu.VMEM((2, KEY_BLOCK, 2 * t), F32),
            scores,
            scores,
            pltpu.VMEM((hps, 1, 2 * t), F32),
            pltpu.VMEM((hps, 1, 2 * t), F32),
            pltpu.VMEM((hps, 1, 2 * t), F32),
            pltpu.VMEM((hps, VT_ROWS, 2 * t), F32),
        ],
        compiler_params=pltpu.CompilerParams(
            dimension_semantics=("arbitrary", "arbitrary", "arbitrary"),
            vmem_limit_bytes=VMEM_LIMIT_BYTES),
    )(q, k, kfeat, cfeat, vt, lam, subln_g)


def _out_mlp_kernel(x_ref, gaya_ref, gb_ref, on_ref, w_b_ref, w_o_ref, g_mlp_ref,
                    w1_ref, w2_ref, g_fin_ref, out_ref, *, final_norm):
    y_b = _dot(on_ref[...], w_b_ref[...])
    merged = gaya_ref[...].astype(F32) + gb_ref[...].astype(F32) * y_b
    x1 = x_ref[...] + _dot(merged.astype(BF16), w_o_ref[...])
    xn = ((x1 * _rms_scale(x1)) * g_mlp_ref[...]).astype(BF16)
    d_ff = w1_ref.shape[1]
    acc = x1
    for c in range(0, d_ff, FF_CHUNK):
        hc = jnp.maximum(_dot(xn, w1_ref[:, c:c + FF_CHUNK]), 0.0)
        acc = acc + _dot((hc * hc).astype(BF16), w2_ref[c:c + FF_CHUNK, :])
    if final_norm:
        acc = (acc * _rms_scale(acc)) * g_fin_ref[...]
    out_ref[...] = acc


def _out_mlp(x2d, gaya, gb, on, w_b, w_o, g_mlp, w1, w2, g_fin, *, final_norm):
    n, d_model = x2d.shape
    tm = TOKEN_TILE
    assert n % tm == 0 and w1.shape[1] % FF_CHUNK == 0
    row = lambda i: (i, 0)
    return pl.pallas_call(
        functools.partial(_out_mlp_kernel, final_norm=final_norm),
        name="out_mlp",
        grid=(n // tm,),
        in_specs=[
            pl.BlockSpec((tm, d_model), row),
            pl.BlockSpec((tm, d_model), row),
            pl.BlockSpec((tm, d_model), row),
            pl.BlockSpec((tm, on.shape[1]), row),
            _resident(w_b.shape),
            _resident(w_o.shape),
            _resident(g_mlp.shape),
            _resident(w1.shape),
            _resident(w2.shape),
            _resident(g_fin.shape),
        ],
        out_specs=pl.BlockSpec((tm, d_model), row),
        out_shape=jax.ShapeDtypeStruct((n, d_model), F32),
        compiler_params=pltpu.CompilerParams(
            dimension_semantics=("arbitrary",), vmem_limit_bytes=VMEM_LIMIT_BYTES),
    )(x2d, gaya, gb, on, w_b, w_o, g_mlp, w1, w2, g_fin)


def _lambda_init(layer_idx):
    return 0.8 - 0.6 * math.exp(-0.3 * layer_idx)


def kernel(x, norm_mix_g, w_in, b_gate, conv_w, lambda_q1, lambda_k1, lambda_q2, lambda_k2,
           subln_g, w_a_out, w_b_out, w_o, norm_mlp_g, w_mlp_in, w_mlp_out, norm_final_g):
    batch, seq, d_model = x.shape
    depth = w_in.shape[0]
    x2d = x.reshape(batch * seq, d_model)
    row = lambda a: a.reshape(1, -1)
    for l in range(depth):
        lam_init = _lambda_init(l)
        q, k, vt, gaya, gb, lam = _in_proj(
            x2d, row(norm_mix_g[l]), w_in[l].astype(BF16), row(b_gate[l]), conv_w[l],
            w_a_out[l].astype(BF16), row(lambda_q1[l]), row(lambda_k1[l]),
            row(lambda_q2[l]), row(lambda_k2[l]), seq=seq, lam_init=lam_init)
        on = _diff_attention(q, k, vt, lam, row(subln_g[l]), batch=batch, seq=seq,
                             out_scale=1.0 - lam_init)
        x2d = _out_mlp(
            x2d, gaya, gb, on, w_b_out[l].astype(BF16), w_o[l].astype(BF16),
            row(norm_mlp_g[l]), w_mlp_in[l].astype(BF16), w_mlp_out[l].astype(BF16),
            row(norm_final_g), final_norm=(l == depth - 1))
    return x2d.reshape(batch, seq, d_model)
```

```python
import functools
import math

import jax
import jax.numpy as jnp
import numpy as np
from jax import lax
from jax.experimental import pallas as pl
from jax.experimental.pallas import tpu as pltpu

F32 = jnp.float32
BF16 = jnp.bfloat16

RMS_EPS = 1e-6
CONV_K = 3
DIFF_HEAD_DIM = 64
DIFF_V_DIM = 2 * DIFF_HEAD_DIM

LOG2E = math.log2(math.e)
LANES = 128
SUM_ROWS = 16
VT_ROWS = DIFF_V_DIM + SUM_ROWS
SUBLANES = 8
VMEM_LIMIT_BYTES = 56 * 1024 * 1024

TOKEN_TILE = 512
FF_CHUNK = 1024
CAST_CHUNK = 512
HEADS_PER_STEP = 4
Q_TILE = 256
KEY_BLOCK = 2 * Q_TILE
POS_SPLIT = 16
SLOPE_PIECES = 3


def _rms_scale(x):
    return lax.rsqrt(jnp.mean(x * x, axis=-1, keepdims=True) + RMS_EPS)


def _dot(a, b):
    return jnp.dot(a, b, preferred_element_type=F32)


def _dot_nt(a, b):
    return lax.dot_general(a, b, (((1,), (1,)), ((), ())), preferred_element_type=F32)


def _resident(shape):
    return pl.BlockSpec(shape, lambda *_: (0,) * len(shape), pipeline_mode=pl.Buffered(1))


def _in_proj_kernel(x0_ref, x_next_ref, g_ref, w_in_ref, b_gate_ref, conv_w_ref, w_a_ref,
                    lq1_ref, lk1_ref, lq2_ref, lk2_ref,
                    q_ref, k_ref, vt_ref, gaya_ref, gb_ref, lam_ref,
                    h_ref, w_vt_ref, xn_ref, w_a_bf_ref, w_bf_ref, *, tiles_per_seq, conv_width,
                    diff_width, d_model, lam_init):
    i = pl.program_id(0)
    tm = xn_ref.shape[0]
    c0 = conv_width
    q0 = 3 * conv_width
    g0 = q0 + 3 * diff_width

    def normed(x_ref):
        x = x_ref[...]
        return ((x * _rms_scale(x)) * g_ref[...]).astype(BF16)

    @pl.when(i == 0)
    def _():
        lam = (jnp.exp(jnp.sum(lq1_ref[...] * lk1_ref[...], axis=-1, keepdims=True))
               - jnp.exp(jnp.sum(lq2_ref[...] * lk2_ref[...], axis=-1, keepdims=True))
               + lam_init)
        lam_ref[...] = jnp.broadcast_to(lam, lam_ref.shape)
        v0 = q0 + 2 * diff_width
        for c in range(0, diff_width, LANES):
            w_vt_ref[c:c + LANES, :] = jnp.transpose(
                w_in_ref[:, v0 + c:v0 + c + LANES]).astype(BF16)
        xn_ref[...] = normed(x0_ref)
        w_a_bf_ref[...] = w_a_ref[...].astype(BF16)
        for c in range(0, w_in_ref.shape[1], CAST_CHUNK):
            w_bf_ref[:, c:c + CAST_CHUNK] = w_in_ref[:, c:c + CAST_CHUNK].astype(BF16)

    xn = xn_ref[...]

    uc = _dot(xn, w_bf_ref[:, 0:q0])
    h = uc[:, c0:2 * c0] * uc[:, 2 * c0:3 * c0]

    @pl.when(i % tiles_per_seq == 0)
    def _():
        h_ref[0:SUBLANES, :] = jnp.zeros((SUBLANES, c0), F32)

    h_ref[SUBLANES:SUBLANES + tm, :] = h
    cw = conv_w_ref[...]
    conv = (cw[0:1, :] * h_ref[SUBLANES - 2:SUBLANES - 2 + tm, :]
            + cw[1:2, :] * h_ref[SUBLANES - 1:SUBLANES - 1 + tm, :]
            + cw[2:3, :] * h)
    h_ref[0:SUBLANES, :] = h[tm - SUBLANES:tm, :]
    gated = (uc[:, 0:c0] * conv).astype(BF16)

    gates = jax.nn.sigmoid(_dot(xn, w_bf_ref[:, g0:g0 + 2 * d_model]) + b_gate_ref[...])
    gb_ref[...] = gates[:, d_model:2 * d_model].astype(BF16)

    scale = DIFF_HEAD_DIM ** -0.5 * LOG2E
    q_ref[...] = (_dot(xn, w_bf_ref[:, q0:q0 + diff_width]) * scale).astype(BF16)
    k_ref[...] = _dot(xn, w_bf_ref[:, q0 + diff_width:q0 + 2 * diff_width]).astype(BF16)
    vt = _dot_nt(w_vt_ref[...], xn).astype(BF16)
    for hh in range(diff_width // DIFF_V_DIM):
        r0 = hh * VT_ROWS
        vt_ref[r0:r0 + DIFF_V_DIM, :] = vt[hh * DIFF_V_DIM:(hh + 1) * DIFF_V_DIM, :]
        vt_ref[r0 + DIFF_V_DIM:r0 + VT_ROWS, :] = jnp.ones((SUM_ROWS, tm), BF16)

    gaya_ref[...] = (gates[:, 0:d_model] * _dot(gated, w_a_bf_ref[...])).astype(BF16)

    xn_ref[...] = normed(x_next_ref)


def _in_proj(x2d, g, w_in, b_gate, conv_w, w_a, lq1, lk1, lq2, lk2, *, seq, lam_init):
    n, d_model = x2d.shape
    conv_width = conv_w.shape[1]
    diff_width = (w_in.shape[1] - 3 * conv_width - 2 * d_model) // 3
    tm = TOKEN_TILE
    assert seq % tm == 0 and n % seq == 0 and w_in.shape[1] % CAST_CHUNK == 0
    steps = n // tm
    vt_rows = (diff_width // DIFF_V_DIM) * VT_ROWS
    row = lambda i: (i, 0)
    kern = functools.partial(
        _in_proj_kernel, tiles_per_seq=seq // tm, conv_width=conv_width,
        diff_width=diff_width, d_model=d_model, lam_init=lam_init)
    return pl.pallas_call(
        kern,
        name="in_proj",
        grid=(steps,),
        in_specs=[
            pl.BlockSpec((tm, d_model), lambda i: (0, 0)),
            pl.BlockSpec((tm, d_model), lambda i: (jnp.minimum(i + 1, steps - 1), 0)),
            _resident(g.shape),
            _resident(w_in.shape),
            _resident(b_gate.shape),
            _resident(conv_w.shape),
            _resident(w_a.shape),
            _resident(lq1.shape), _resident(lk1.shape), _resident(lq2.shape), _resident(lk2.shape),
        ],
        out_specs=[
            pl.BlockSpec((tm, diff_width), row),
            pl.BlockSpec((tm, diff_width), row),
            pl.BlockSpec((vt_rows, tm), lambda i: (0, i)),
            pl.BlockSpec((tm, d_model), row),
            pl.BlockSpec((tm, d_model), row),
            pl.BlockSpec((SUBLANES, LANES), lambda i: (0, 0)),
        ],
        out_shape=[
            jax.ShapeDtypeStruct((n, diff_width), BF16),
            jax.ShapeDtypeStruct((n, diff_width), BF16),
            jax.ShapeDtypeStruct((vt_rows, n), BF16),
            jax.ShapeDtypeStruct((n, d_model), BF16),
            jax.ShapeDtypeStruct((n, d_model), BF16),
            jax.ShapeDtypeStruct((SUBLANES, LANES), F32),
        ],
        scratch_shapes=[
            pltpu.VMEM((SUBLANES + tm, conv_width), F32),
            pltpu.VMEM((diff_width, d_model), BF16),
            pltpu.VMEM((tm, d_model), BF16),
            pltpu.VMEM(w_a.shape, BF16),
            pltpu.VMEM(w_in.shape, BF16),
        ],
        compiler_params=pltpu.CompilerParams(
            dimension_semantics=("arbitrary",), vmem_limit_bytes=VMEM_LIMIT_BYTES),
    )(x2d, x2d, g, w_in, b_gate, conv_w, w_a, lq1, lk1, lq2, lk2)


def _alibi_features(seq, heads):
    pos = np.arange(seq)
    kfeat = np.zeros((seq, LANES), np.float32)
    cfeat = np.zeros((heads, LANES), np.float32)
    as_bf16 = lambda a: a.astype(BF16).astype(np.float32)
    slopes = np.array([2.0 ** (-8.0 * (h + 1) / heads) for h in range(heads)], np.float32)
    rest = (slopes * np.float32(LOG2E)).astype(np.float32)
    for piece in range(SLOPE_PIECES):
        part = as_bf16(rest)
        rest = (rest - part).astype(np.float32)
        kfeat[:, 2 * piece] = POS_SPLIT * (pos // POS_SPLIT)
        kfeat[:, 2 * piece + 1] = pos % POS_SPLIT
        cfeat[:, 2 * piece] = part
        cfeat[:, 2 * piece + 1] = part
    assert seq <= POS_SPLIT * 256 and np.array_equal(as_bf16(kfeat), kfeat)
    return jnp.asarray(kfeat, BF16), jnp.asarray(cfeat, F32)


def _attn_kernel(q_ref, k_ref, kfeat_ref, cfeat_ref, vt_ref, lam_ref, g_ref, o_ref,
                 q2_ref, mask_ref, sx_ref, sy_ref, sx_max_ref, sy_max_ref, m_ref, acc_ref,
                 *, out_scale):
    hg = pl.program_id(1)
    i = pl.program_id(2)
    last_tile = pl.num_programs(2) - 1
    t = Q_TILE
    kb = KEY_BLOCK
    d = DIFF_HEAD_DIM
    vd = DIFF_V_DIM
    hps = q2_ref.shape[0]
    sx = (sx_ref, sx_max_ref)
    sy = (sy_ref, sy_max_ref)
    nb = i // 2
    parity = i % 2

    def build_q2(tile):
        rows = pl.ds(pl.multiple_of(tile * t, t), t)
        lane = lax.broadcasted_iota(jnp.int32, (t, vd), 1)
        for hh in range(hps):
            q = q_ref[rows, hh * vd:(hh + 1) * vd]
            zero = jnp.zeros_like(q)
            q2_ref[hh, 0:t, 0:LANES] = jnp.where(lane < d, q, zero)
            q2_ref[hh, t:2 * t, 0:LANES] = jnp.where(lane >= d, q, zero)

    def block_start(b):
        return pl.multiple_of(b * kb, kb)

    def raw_scores(dst, b):
        dst_ref, dst_max_ref = dst
        start = block_start(b)
        feat = kfeat_ref[pl.ds(start, kb), :]
        for hh in range(hps):
            keys = jnp.concatenate([k_ref[pl.ds(start, kb), hh * vd:(hh + 1) * vd], feat], axis=1)
            s = _dot_nt(keys, q2_ref[hh])
            dst_ref[hh] = s
            dst_max_ref[hh] = jnp.max(s, axis=0, keepdims=True)

    def consume(src, b, mask=None, rows=kb):
        src_ref, src_max_ref = src
        start = block_start(b)
        for hh in range(hps):
            s = src_ref[hh, 0:rows, :]
            if mask is None:
                s_max = src_max_ref[hh]
            else:
                s = s + mask_ref[mask, 0:rows, :]
                s_max = jnp.max(s, axis=0, keepdims=True)
            m_old = m_ref[hh]
            m_new = jnp.maximum(m_old, s_max)
            alpha = jnp.exp2(m_old - m_new)
            p = jnp.exp2(s - m_new)
            m_ref[hh] = m_new
            pv = _dot(vt_ref[hh * VT_ROWS:(hh + 1) * VT_ROWS, pl.ds(start, rows)], p.astype(BF16))
            acc_ref[hh] = alpha * acc_ref[hh] + pv

    def consume_masked(src, b, diag_first):
        if diag_first:
            consume(src, b, mask=0, rows=t)
        else:
            consume(src, b, mask=1)

    @pl.when(i == 0)
    def _():
        key = lax.broadcasted_iota(jnp.int32, (kb, 2 * t), 0)
        qry = lax.broadcasted_iota(jnp.int32, (kb, 2 * t), 1)
        qry = jnp.where(qry >= t, qry - t, qry)
        zero = jnp.zeros((kb, 2 * t), F32)
        mask_ref[0] = jnp.where(key <= qry, zero, -jnp.inf)
        mask_ref[1] = jnp.where(key - t <= qry, zero, -jnp.inf)
        for hh in range(hps):
            row = cfeat_ref[pl.ds(hg * hps + hh, 1), :]
            q2_ref[hh, :, LANES:2 * LANES] = jnp.broadcast_to(row, (2 * t, LANES)).astype(BF16)
        build_q2(0)
        raw_scores(sx, 0)

    for hh in range(hps):
        m_ref[hh] = jnp.full((1, 2 * t), -jnp.inf, F32)
        acc_ref[hh] = jnp.zeros((VT_ROWS, 2 * t), F32)

    def body(pair, carry):
        b = 2 * pair
        raw_scores(sy, b + 1)
        consume(sx, b)
        raw_scores(sx, b + 2)
        consume(sy, b + 1)
        return carry

    lax.fori_loop(0, nb // 2, body, 0)

    for diag_first in (True, False):
        @pl.when((nb % 2 == 1) & (parity == (0 if diag_first else 1)))
        def _():
            raw_scores(sy, nb)
            consume(sx, nb - 1)
            consume_masked(sy, nb, diag_first)

        @pl.when((nb % 2 == 0) & (parity == (0 if diag_first else 1)))
        def _():
            consume_masked(sx, nb, diag_first)

    build_q2(jnp.minimum(i + 1, last_tile))
    raw_scores(sx, 0)

    lam = lam_ref[0:1, 0:1]
    for hh in range(hps):
        on = acc_ref[hh, 0:vd, :] * (1.0 / acc_ref[hh, vd:vd + 1, :])
        o = jnp.transpose(on[:, 0:t] - lam * on[:, t:2 * t])
        o = ((o * _rms_scale(o)) * g_ref[...]) * out_scale
        o_ref[:, hh * vd:(hh + 1) * vd] = o.astype(o_ref.dtype)


def _diff_attention(q, k, vt, lam, subln_g, *, batch, seq, out_scale):
    n, width = q.shape
    heads = width // DIFF_V_DIM
    hps = HEADS_PER_STEP
    t = Q_TILE
    assert seq % KEY_BLOCK == 0 and heads % hps == 0 and DIFF_V_DIM == LANES
    nq = seq // t
    kfeat, cfeat = _alibi_features(seq, heads)
    scores = pltpu.VMEM((hps, KEY_BLOCK, 2 * t), F32)
    per_head_seq = pl.BlockSpec((seq, hps * DIFF_V_DIM), lambda b, h, i: (b, h))
    return pl.pallas_call(
        functools.partial(_attn_kernel, out_scale=out_scale),
        name="diff_attn",
        grid=(batch, heads // hps, nq),
        in_specs=[
            per_head_seq,
            per_head_seq,
            _resident(kfeat.shape),
            _resident(cfeat.shape),
            pl.BlockSpec((hps * VT_ROWS, seq), lambda b, h, i: (h, b)),
            _resident(lam.shape),
            _resident(subln_g.shape),
        ],
        out_specs=pl.BlockSpec((t, hps * DIFF_V_DIM), lambda b, h, i: (b * nq + i, h)),
        out_shape=jax.ShapeDtypeStruct((n, width), BF16),
        scratch_shapes=[
            pltpu.VMEM((hps, 2 * t, 2 * LANES), BF16),
            pltpu.VMEM((2, KEY_BLOCK, 2 * t), F32),
            scores,
            scores,
            pltpu.VMEM((hps, 1, 2 * t), F32),
            pltpu.VMEM((hps, 1, 2 * t), F32),
            pltpu.VMEM((hps, 1, 2 * t), F32),
            pltpu.VMEM((hps, VT_ROWS, 2 * t), F32),
        ],
        compiler_params=pltpu.CompilerParams(
            dimension_semantics=("arbitrary", "arbitrary", "arbitrary"),
            vmem_limit_bytes=VMEM_LIMIT_BYTES),
    )(q, k, kfeat, cfeat, vt, lam, subln_g)


def _out_mlp_kernel(x_ref, gaya_ref, gb_ref, on_ref, w_b_ref, w_o_ref, g_mlp_ref,
                    w1_ref, w2_ref, g_fin_ref, out_ref, w_b_bf_ref, w_o_bf_ref, *, final_norm):
    @pl.when(pl.program_id(0) == 0)
    def _():
        w_b_bf_ref[...] = w_b_ref[...].astype(BF16)
        w_o_bf_ref[...] = w_o_ref[...].astype(BF16)

    y_b = _dot(on_ref[...], w_b_bf_ref[...])
    merged = gaya_ref[...].astype(F32) + gb_ref[...].astype(F32) * y_b
    x1 = x_ref[...] + _dot(merged.astype(BF16), w_o_bf_ref[...])
    xn = ((x1 * _rms_scale(x1)) * g_mlp_ref[...]).astype(BF16)
    d_ff = w1_ref.shape[1]
    acc = x1
    for c in range(0, d_ff, FF_CHUNK):
        hc = jnp.maximum(_dot(xn, w1_ref[:, c:c + FF_CHUNK]), 0.0)
        acc = acc + _dot((hc * hc).astype(BF16), w2_ref[c:c + FF_CHUNK, :])
    if final_norm:
        acc = (acc * _rms_scale(acc)) * g_fin_ref[...]
    out_ref[...] = acc


def _out_mlp(x2d, gaya, gb, on, w_b, w_o, g_mlp, w1, w2, g_fin, *, final_norm):
    n, d_model = x2d.shape
    tm = TOKEN_TILE
    assert n % tm == 0 and w1.shape[1] % FF_CHUNK == 0
    row = lambda i: (i, 0)
    return pl.pallas_call(
        functools.partial(_out_mlp_kernel, final_norm=final_norm),
        name="out_mlp",
        grid=(n // tm,),
        in_specs=[
            pl.BlockSpec((tm, d_model), row),
            pl.BlockSpec((tm, d_model), row),
            pl.BlockSpec((tm, d_model), row),
            pl.BlockSpec((tm, on.shape[1]), row),
            _resident(w_b.shape),
            _resident(w_o.shape),
            _resident(g_mlp.shape),
            _resident(w1.shape),
            _resident(w2.shape),
            _resident(g_fin.shape),
        ],
        out_specs=pl.BlockSpec((tm, d_model), row),
        out_shape=jax.ShapeDtypeStruct((n, d_model), F32),
        scratch_shapes=[
            pltpu.VMEM(w_b.shape, BF16),
            pltpu.VMEM(w_o.shape, BF16),
        ],
        compiler_params=pltpu.CompilerParams(
            dimension_semantics=("arbitrary",), vmem_limit_bytes=VMEM_LIMIT_BYTES),
    )(x2d, gaya, gb, on, w_b, w_o, g_mlp, w1, w2, g_fin)


def _lambda_init(layer_idx):
    return 0.8 - 0.6 * math.exp(-0.3 * layer_idx)


def kernel(x, norm_mix_g, w_in, b_gate, conv_w, lambda_q1, lambda_k1, lambda_q2, lambda_k2,
           subln_g, w_a_out, w_b_out, w_o, norm_mlp_g, w_mlp_in, w_mlp_out, norm_final_g):
    batch, seq, d_model = x.shape
    depth = w_in.shape[0]
    x2d = x.reshape(batch * seq, d_model)
    row = lambda a: a.reshape(1, -1)
    for l in range(depth):
        lam_init = _lambda_init(l)
        q, k, vt, gaya, gb, lam = _in_proj(
            x2d, row(norm_mix_g[l]), w_in[l], row(b_gate[l]), conv_w[l],
            w_a_out[l], row(lambda_q1[l]), row(lambda_k1[l]),
            row(lambda_q2[l]), row(lambda_k2[l]), seq=seq, lam_init=lam_init)
        on = _diff_attention(q, k, vt, lam, row(subln_g[l]), batch=batch, seq=seq,
                             out_scale=1.0 - lam_init)
        x2d = _out_mlp(
            x2d, gaya, gb, on, w_b_out[l], w_o[l],
            row(norm_mlp_g[l]), w_mlp_in[l].astype(BF16), w_mlp_out[l].astype(BF16),
            row(norm_final_g), final_norm=(l == depth - 1))
    return x2d.reshape(batch, seq, d_model)
```

```python
import functools
import math

import jax
import jax.numpy as jnp
import numpy as np
from jax import lax
from jax.experimental import pallas as pl
from jax.experimental.pallas import tpu as pltpu

F32 = jnp.float32
BF16 = jnp.bfloat16

RMS_EPS = 1e-6
CONV_K = 3
DIFF_HEAD_DIM = 64
DIFF_V_DIM = 2 * DIFF_HEAD_DIM

LOG2E = math.log2(math.e)
LANES = 128
SUM_ROWS = 16
VT_ROWS = DIFF_V_DIM + SUM_ROWS
SUBLANES = 8
VMEM_LIMIT_BYTES = 56 * 1024 * 1024

TOKEN_TILE = 512
FF_CHUNK = 1024
CAST_CHUNK = 512
STAGE_BYTES = 1024 * 1024
HEADS_PER_STEP = 4
Q_TILE = 256
KEY_BLOCK = 2 * Q_TILE
POS_SPLIT = 16
SLOPE_PIECES = 3


def _rms_scale(x):
    return lax.rsqrt(jnp.mean(x * x, axis=-1, keepdims=True) + RMS_EPS)


def _dot(a, b):
    return jnp.dot(a, b, preferred_element_type=F32)


def _dot_nt(a, b):
    return lax.dot_general(a, b, (((1,), (1,)), ((), ())), preferred_element_type=F32)


def _resident(shape):
    return pl.BlockSpec(shape, lambda *_: (0,) * len(shape), pipeline_mode=pl.Buffered(1))


def _in_proj_kernel(x0_ref, x_next_ref, g_ref, w_in_ref, b_gate_ref, conv_w_ref, w_a_ref,
                    lq1_ref, lk1_ref, lq2_ref, lk2_ref,
                    q_ref, k_ref, vt_ref, gaya_ref, gb_ref, lam_ref,
                    h_ref, w_vt_ref, xn_ref, w_a_bf_ref, w_bf_ref, *, tiles_per_seq, conv_width,
                    diff_width, d_model, lam_init):
    i = pl.program_id(0)
    tm = xn_ref.shape[0]
    c0 = conv_width
    q0 = 3 * conv_width
    g0 = q0 + 3 * diff_width

    def normed(x_ref):
        x = x_ref[...]
        return ((x * _rms_scale(x)) * g_ref[...]).astype(BF16)

    @pl.when(i == 0)
    def _():
        lam = (jnp.exp(jnp.sum(lq1_ref[...] * lk1_ref[...], axis=-1, keepdims=True))
               - jnp.exp(jnp.sum(lq2_ref[...] * lk2_ref[...], axis=-1, keepdims=True))
               + lam_init)
        lam_ref[...] = jnp.broadcast_to(lam, lam_ref.shape)
        v0 = q0 + 2 * diff_width
        for c in range(0, diff_width, LANES):
            w_vt_ref[c:c + LANES, :] = jnp.transpose(
                w_in_ref[:, v0 + c:v0 + c + LANES]).astype(BF16)
        xn_ref[...] = normed(x0_ref)
        w_a_bf_ref[...] = w_a_ref[...].astype(BF16)
        for c in range(0, w_in_ref.shape[1], CAST_CHUNK):
            w_bf_ref[:, c:c + CAST_CHUNK] = w_in_ref[:, c:c + CAST_CHUNK].astype(BF16)

    xn = xn_ref[...]

    uc = _dot(xn, w_bf_ref[:, 0:q0])
    h = uc[:, c0:2 * c0] * uc[:, 2 * c0:3 * c0]

    @pl.when(i % tiles_per_seq == 0)
    def _():
        h_ref[0:SUBLANES, :] = jnp.zeros((SUBLANES, c0), F32)

    h_ref[SUBLANES:SUBLANES + tm, :] = h
    cw = conv_w_ref[...]
    conv = (cw[0:1, :] * h_ref[SUBLANES - 2:SUBLANES - 2 + tm, :]
            + cw[1:2, :] * h_ref[SUBLANES - 1:SUBLANES - 1 + tm, :]
            + cw[2:3, :] * h)
    h_ref[0:SUBLANES, :] = h[tm - SUBLANES:tm, :]
    gated = (uc[:, 0:c0] * conv).astype(BF16)

    gates = jax.nn.sigmoid(_dot(xn, w_bf_ref[:, g0:g0 + 2 * d_model]) + b_gate_ref[...])
    gb_ref[...] = gates[:, d_model:2 * d_model].astype(BF16)

    scale = DIFF_HEAD_DIM ** -0.5 * LOG2E
    q_ref[...] = (_dot(xn, w_bf_ref[:, q0:q0 + diff_width]) * scale).astype(BF16)
    k_ref[...] = _dot(xn, w_bf_ref[:, q0 + diff_width:q0 + 2 * diff_width]).astype(BF16)
    vt = _dot_nt(w_vt_ref[...], xn).astype(BF16)
    for hh in range(diff_width // DIFF_V_DIM):
        r0 = hh * VT_ROWS
        vt_ref[r0:r0 + DIFF_V_DIM, :] = vt[hh * DIFF_V_DIM:(hh + 1) * DIFF_V_DIM, :]
        vt_ref[r0 + DIFF_V_DIM:r0 + VT_ROWS, :] = jnp.ones((SUM_ROWS, tm), BF16)

    gaya_ref[...] = (gates[:, 0:d_model] * _dot(gated, w_a_bf_ref[...])).astype(BF16)

    xn_ref[...] = normed(x_next_ref)


def _in_proj(x2d, g, w_in, b_gate, conv_w, w_a, lq1, lk1, lq2, lk2, *, seq, lam_init):
    n, d_model = x2d.shape
    conv_width = conv_w.shape[1]
    diff_width = (w_in.shape[1] - 3 * conv_width - 2 * d_model) // 3
    tm = TOKEN_TILE
    assert seq % tm == 0 and n % seq == 0 and w_in.shape[1] % CAST_CHUNK == 0
    steps = n // tm
    vt_rows = (diff_width // DIFF_V_DIM) * VT_ROWS
    row = lambda i: (i, 0)
    kern = functools.partial(
        _in_proj_kernel, tiles_per_seq=seq // tm, conv_width=conv_width,
        diff_width=diff_width, d_model=d_model, lam_init=lam_init)
    return pl.pallas_call(
        kern,
        name="in_proj",
        grid=(steps,),
        in_specs=[
            pl.BlockSpec((tm, d_model), lambda i: (0, 0)),
            pl.BlockSpec((tm, d_model), lambda i: (jnp.minimum(i + 1, steps - 1), 0)),
            _resident(g.shape),
            _resident(w_in.shape),
            _resident(b_gate.shape),
            _resident(conv_w.shape),
            _resident(w_a.shape),
            _resident(lq1.shape), _resident(lk1.shape), _resident(lq2.shape), _resident(lk2.shape),
        ],
        out_specs=[
            pl.BlockSpec((tm, diff_width), row),
            pl.BlockSpec((tm, diff_width), row),
            pl.BlockSpec((vt_rows, tm), lambda i: (0, i)),
            pl.BlockSpec((tm, d_model), row),
            pl.BlockSpec((tm, d_model), row),
            pl.BlockSpec((SUBLANES, LANES), lambda i: (0, 0)),
        ],
        out_shape=[
            jax.ShapeDtypeStruct((n, diff_width), BF16),
            jax.ShapeDtypeStruct((n, diff_width), BF16),
            jax.ShapeDtypeStruct((vt_rows, n), BF16),
            jax.ShapeDtypeStruct((n, d_model), BF16),
            jax.ShapeDtypeStruct((n, d_model), BF16),
            jax.ShapeDtypeStruct((SUBLANES, LANES), F32),
        ],
        scratch_shapes=[
            pltpu.VMEM((SUBLANES + tm, conv_width), F32),
            pltpu.VMEM((diff_width, d_model), BF16),
            pltpu.VMEM((tm, d_model), BF16),
            pltpu.VMEM(w_a.shape, BF16),
            pltpu.VMEM(w_in.shape, BF16),
        ],
        compiler_params=pltpu.CompilerParams(
            dimension_semantics=("arbitrary",), vmem_limit_bytes=VMEM_LIMIT_BYTES),
    )(x2d, x2d, g, w_in, b_gate, conv_w, w_a, lq1, lk1, lq2, lk2)


def _alibi_features(seq, heads):
    pos = np.arange(seq)
    kfeat = np.zeros((seq, LANES), np.float32)
    cfeat = np.zeros((heads, LANES), np.float32)
    as_bf16 = lambda a: a.astype(BF16).astype(np.float32)
    slopes = np.array([2.0 ** (-8.0 * (h + 1) / heads) for h in range(heads)], np.float32)
    rest = (slopes * np.float32(LOG2E)).astype(np.float32)
    for piece in range(SLOPE_PIECES):
        part = as_bf16(rest)
        rest = (rest - part).astype(np.float32)
        kfeat[:, 2 * piece] = POS_SPLIT * (pos // POS_SPLIT)
        kfeat[:, 2 * piece + 1] = pos % POS_SPLIT
        cfeat[:, 2 * piece] = part
        cfeat[:, 2 * piece + 1] = part
    assert seq <= POS_SPLIT * 256 and np.array_equal(as_bf16(kfeat), kfeat)
    return jnp.asarray(kfeat, BF16), jnp.asarray(cfeat, F32)


def _attn_kernel(q_ref, k_ref, kfeat_ref, cfeat_ref, vt_ref, lam_ref, g_ref, o_ref,
                 q2_ref, mask_ref, sx_ref, sy_ref, sx_max_ref, sy_max_ref, m_ref, acc_ref,
                 *, out_scale):
    hg = pl.program_id(1)
    i = pl.program_id(2)
    last_tile = pl.num_programs(2) - 1
    t = Q_TILE
    kb = KEY_BLOCK
    d = DIFF_HEAD_DIM
    vd = DIFF_V_DIM
    hps = q2_ref.shape[0]
    sx = (sx_ref, sx_max_ref)
    sy = (sy_ref, sy_max_ref)
    nb = i // 2
    parity = i % 2

    def build_q2(tile):
        rows = pl.ds(pl.multiple_of(tile * t, t), t)
        lane = lax.broadcasted_iota(jnp.int32, (t, vd), 1)
        for hh in range(hps):
            q = q_ref[rows, hh * vd:(hh + 1) * vd]
            zero = jnp.zeros_like(q)
            q2_ref[hh, 0:t, 0:LANES] = jnp.where(lane < d, q, zero)
            q2_ref[hh, t:2 * t, 0:LANES] = jnp.where(lane >= d, q, zero)

    def block_start(b):
        return pl.multiple_of(b * kb, kb)

    def raw_scores(dst, b):
        dst_ref, dst_max_ref = dst
        start = block_start(b)
        feat = kfeat_ref[pl.ds(start, kb), :]
        for hh in range(hps):
            keys = jnp.concatenate([k_ref[pl.ds(start, kb), hh * vd:(hh + 1) * vd], feat], axis=1)
            s = _dot_nt(keys, q2_ref[hh])
            dst_ref[hh] = s
            dst_max_ref[hh] = jnp.max(s, axis=0, keepdims=True)

    def consume(src, b, mask=None, rows=kb):
        src_ref, src_max_ref = src
        start = block_start(b)
        for hh in range(hps):
            s = src_ref[hh, 0:rows, :]
            if mask is None:
                s_max = src_max_ref[hh]
            else:
                s = s + mask_ref[mask, 0:rows, :]
                s_max = jnp.max(s, axis=0, keepdims=True)
            m_old = m_ref[hh]
            m_new = jnp.maximum(m_old, s_max)
            alpha = jnp.exp2(m_old - m_new)
            p = jnp.exp2(s - m_new)
            m_ref[hh] = m_new
            pv = _dot(vt_ref[hh * VT_ROWS:(hh + 1) * VT_ROWS, pl.ds(start, rows)], p.astype(BF16))
            acc_ref[hh] = alpha * acc_ref[hh] + pv

    def consume_masked(src, b, diag_first):
        if diag_first:
            consume(src, b, mask=0, rows=t)
        else:
            consume(src, b, mask=1)

    @pl.when(i == 0)
    def _():
        key = lax.broadcasted_iota(jnp.int32, (kb, 2 * t), 0)
        qry = lax.broadcasted_iota(jnp.int32, (kb, 2 * t), 1)
        qry = jnp.where(qry >= t, qry - t, qry)
        zero = jnp.zeros((kb, 2 * t), F32)
        mask_ref[0] = jnp.where(key <= qry, zero, -jnp.inf)
        mask_ref[1] = jnp.where(key - t <= qry, zero, -jnp.inf)
        for hh in range(hps):
            row = cfeat_ref[pl.ds(hg * hps + hh, 1), :]
            q2_ref[hh, :, LANES:2 * LANES] = jnp.broadcast_to(row, (2 * t, LANES)).astype(BF16)
        build_q2(0)
        raw_scores(sx, 0)

    for hh in range(hps):
        m_ref[hh] = jnp.full((1, 2 * t), -jnp.inf, F32)
        acc_ref[hh] = jnp.zeros((VT_ROWS, 2 * t), F32)

    def body(pair, carry):
        b = 2 * pair
        raw_scores(sy, b + 1)
        consume(sx, b)
        raw_scores(sx, b + 2)
        consume(sy, b + 1)
        return carry

    lax.fori_loop(0, nb // 2, body, 0)

    for diag_first in (True, False):
        @pl.when((nb % 2 == 1) & (parity == (0 if diag_first else 1)))
        def _():
            raw_scores(sy, nb)
            consume(sx, nb - 1)
            consume_masked(sy, nb, diag_first)

        @pl.when((nb % 2 == 0) & (parity == (0 if diag_first else 1)))
        def _():
            consume_masked(sx, nb, diag_first)

    build_q2(jnp.minimum(i + 1, last_tile))
    raw_scores(sx, 0)

    lam = lam_ref[0:1, 0:1]
    for hh in range(hps):
        on = acc_ref[hh, 0:vd, :] * (1.0 / acc_ref[hh, vd:vd + 1, :])
        o = jnp.transpose(on[:, 0:t] - lam * on[:, t:2 * t])
        o = ((o * _rms_scale(o)) * g_ref[...]) * out_scale
        o_ref[:, hh * vd:(hh + 1) * vd] = o.astype(o_ref.dtype)


def _diff_attention(q, k, vt, lam, subln_g, *, batch, seq, out_scale):
    n, width = q.shape
    heads = width // DIFF_V_DIM
    hps = HEADS_PER_STEP
    t = Q_TILE
    assert seq % KEY_BLOCK == 0 and heads % hps == 0 and DIFF_V_DIM == LANES
    nq = seq // t
    kfeat, cfeat = _alibi_features(seq, heads)
    scores = pltpu.VMEM((hps, KEY_BLOCK, 2 * t), F32)
    per_head_seq = pl.BlockSpec((seq, hps * DIFF_V_DIM), lambda b, h, i: (b, h))
    return pl.pallas_call(
        functools.partial(_attn_kernel, out_scale=out_scale),
        name="diff_attn",
        grid=(batch, heads // hps, nq),
        in_specs=[
            per_head_seq,
            per_head_seq,
            _resident(kfeat.shape),
            _resident(cfeat.shape),
            pl.BlockSpec((hps * VT_ROWS, seq), lambda b, h, i: (h, b)),
            _resident(lam.shape),
            _resident(subln_g.shape),
        ],
        out_specs=pl.BlockSpec((t, hps * DIFF_V_DIM), lambda b, h, i: (b * nq + i, h)),
        out_shape=jax.ShapeDtypeStruct((n, width), BF16),
        scratch_shapes=[
            pltpu.VMEM((hps, 2 * t, 2 * LANES), BF16),
            pltpu.VMEM((2, KEY_BLOCK, 2 * t), F32),
            scores,
            scores,
            pltpu.VMEM((hps, 1, 2 * t), F32),
            pltpu.VMEM((hps, 1, 2 * t), F32),
            pltpu.VMEM((hps, 1, 2 * t), F32),
            pltpu.VMEM((hps, VT_ROWS, 2 * t), F32),
        ],
        compiler_params=pltpu.CompilerParams(
            dimension_semantics=("arbitrary", "arbitrary", "arbitrary"),
            vmem_limit_bytes=VMEM_LIMIT_BYTES),
    )(q, k, kfeat, cfeat, vt, lam, subln_g)


def _fetch_rounded(src_hbm, stage_ref, sem_ref, dst_ref):
    rows = stage_ref.shape[1]
    chunks = src_hbm.shape[0] // rows

    def copy(j):
        return pltpu.make_async_copy(
            src_hbm.at[pl.ds(j * rows, rows)], stage_ref.at[j % 2], sem_ref.at[j % 2])

    copy(0).start()
    for j in range(chunks):
        if j + 1 < chunks:
            copy(j + 1).start()
        copy(j).wait()
        dst_ref[j * rows:(j + 1) * rows, :] = stage_ref[j % 2].astype(BF16)


def _out_mlp_kernel(x_ref, gaya_ref, gb_ref, on_ref, w_b_ref, w_o_ref, g_mlp_ref,
                    w1_hbm, w2_hbm, g_fin_ref, out_ref, w_b_bf_ref, w_o_bf_ref, w1_ref, w2_ref,
                    stage1_ref, stage2_ref, sem_ref, *, final_norm):
    @pl.when(pl.program_id(0) == 0)
    def _():
        w_b_bf_ref[...] = w_b_ref[...].astype(BF16)
        w_o_bf_ref[...] = w_o_ref[...].astype(BF16)
        _fetch_rounded(w1_hbm, stage1_ref, sem_ref, w1_ref)
        _fetch_rounded(w2_hbm, stage2_ref, sem_ref, w2_ref)

    y_b = _dot(on_ref[...], w_b_bf_ref[...])
    merged = gaya_ref[...].astype(F32) + gb_ref[...].astype(F32) * y_b
    x1 = x_ref[...] + _dot(merged.astype(BF16), w_o_bf_ref[...])
    xn = ((x1 * _rms_scale(x1)) * g_mlp_ref[...]).astype(BF16)
    d_ff = w1_ref.shape[1]
    acc = x1
    for c in range(0, d_ff, FF_CHUNK):
        hc = jnp.maximum(_dot(xn, w1_ref[:, c:c + FF_CHUNK]), 0.0)
        acc = acc + _dot((hc * hc).astype(BF16), w2_ref[c:c + FF_CHUNK, :])
    if final_norm:
        acc = (acc * _rms_scale(acc)) * g_fin_ref[...]
    out_ref[...] = acc


def _out_mlp(x2d, gaya, gb, on, w_b, w_o, g_mlp, w1, w2, g_fin, *, final_norm):
    n, d_model = x2d.shape
    tm = TOKEN_TILE
    assert n % tm == 0 and w1.shape[1] % FF_CHUNK == 0
    stage_rows = lambda w: STAGE_BYTES // (w.shape[1] * 4)
    assert w1.shape[0] % stage_rows(w1) == 0 and w2.shape[0] % stage_rows(w2) == 0
    row = lambda i: (i, 0)
    return pl.pallas_call(
        functools.partial(_out_mlp_kernel, final_norm=final_norm),
        name="out_mlp",
        grid=(n // tm,),
        in_specs=[
            pl.BlockSpec((tm, d_model), row),
            pl.BlockSpec((tm, d_model), row),
            pl.BlockSpec((tm, d_model), row),
            pl.BlockSpec((tm, on.shape[1]), row),
            _resident(w_b.shape),
            _resident(w_o.shape),
            _resident(g_mlp.shape),
            pl.BlockSpec(memory_space=pl.ANY),
            pl.BlockSpec(memory_space=pl.ANY),
            _resident(g_fin.shape),
        ],
        out_specs=pl.BlockSpec((tm, d_model), row),
        out_shape=jax.ShapeDtypeStruct((n, d_model), F32),
        scratch_shapes=[
            pltpu.VMEM(w_b.shape, BF16),
            pltpu.VMEM(w_o.shape, BF16),
            pltpu.VMEM(w1.shape, BF16),
            pltpu.VMEM(w2.shape, BF16),
            pltpu.VMEM((2, stage_rows(w1), w1.shape[1]), F32),
            pltpu.VMEM((2, stage_rows(w2), w2.shape[1]), F32),
            pltpu.SemaphoreType.DMA((2,)),
        ],
        compiler_params=pltpu.CompilerParams(
            dimension_semantics=("arbitrary",), vmem_limit_bytes=VMEM_LIMIT_BYTES),
    )(x2d, gaya, gb, on, w_b, w_o, g_mlp, w1, w2, g_fin)


def _lambda_init(layer_idx):
    return 0.8 - 0.6 * math.exp(-0.3 * layer_idx)


def kernel(x, norm_mix_g, w_in, b_gate, conv_w, lambda_q1, lambda_k1, lambda_q2, lambda_k2,
           subln_g, w_a_out, w_b_out, w_o, norm_mlp_g, w_mlp_in, w_mlp_out, norm_final_g):
    batch, seq, d_model = x.shape
    depth = w_in.shape[0]
    x2d = x.reshape(batch * seq, d_model)
    row = lambda a: a.reshape(1, -1)
    for l in range(depth):
        lam_init = _lambda_init(l)
        q, k, vt, gaya, gb, lam = _in_proj(
            x2d, row(norm_mix_g[l]), w_in[l], row(b_gate[l]), conv_w[l],
            w_a_out[l], row(lambda_q1[l]), row(lambda_k1[l]),
            row(lambda_q2[l]), row(lambda_k2[l]), seq=seq, lam_init=lam_init)
        on = _diff_attention(q, k, vt, lam, row(subln_g[l]), batch=batch, seq=seq,
                             out_scale=1.0 - lam_init)
        x2d = _out_mlp(
            x2d, gaya, gb, on, w_b_out[l], w_o[l],
            row(norm_mlp_g[l]), w_mlp_in[l], w_mlp_out[l],
            row(norm_final_g), final_norm=(l == depth - 1))
    return x2d.reshape(batch, seq, d_model)
```

```python
import functools
import math

import jax
import jax.numpy as jnp
import numpy as np
from jax import lax
from jax.experimental import pallas as pl
from jax.experimental.pallas import tpu as pltpu

F32 = jnp.float32
BF16 = jnp.bfloat16

RMS_EPS = 1e-6
CONV_K = 3
DIFF_HEAD_DIM = 64
DIFF_V_DIM = 2 * DIFF_HEAD_DIM

LOG2E = math.log2(math.e)
LANES = 128
SUM_ROWS = 16
VT_ROWS = DIFF_V_DIM + SUM_ROWS
SUBLANES = 8
VMEM_LIMIT_BYTES = 56 * 1024 * 1024

TOKEN_TILE = 512
FF_CHUNK = 1024
CAST_CHUNK = 512
HEADS_PER_STEP = 4
Q_TILE = 256
KEY_BLOCK = 2 * Q_TILE
POS_SPLIT = 16
SLOPE_PIECES = 3


def _rms_scale(x):
    return lax.rsqrt(jnp.mean(x * x, axis=-1, keepdims=True) + RMS_EPS)


def _dot(a, b):
    return jnp.dot(a, b, preferred_element_type=F32)


def _dot_nt(a, b):
    return lax.dot_general(a, b, (((1,), (1,)), ((), ())), preferred_element_type=F32)


def _resident(shape):
    return pl.BlockSpec(shape, lambda *_: (0,) * len(shape), pipeline_mode=pl.Buffered(1))


def _in_proj_kernel(x0_ref, x_next_ref, g_ref, w_in_ref, b_gate_ref, conv_w_ref, w_a_ref,
                    lq1_ref, lk1_ref, lq2_ref, lk2_ref,
                    q_ref, k_ref, vt_ref, gaya_ref, gb_ref, lam_ref,
                    h_ref, w_vt_ref, xn_ref, w_a_bf_ref, w_bf_ref, *, tiles_per_seq, conv_width,
                    diff_width, d_model, lam_init):
    i = pl.program_id(0)
    tm = xn_ref.shape[0]
    c0 = conv_width
    q0 = 3 * conv_width
    g0 = q0 + 3 * diff_width

    def normed(x_ref):
        x = x_ref[...]
        return ((x * _rms_scale(x)) * g_ref[...]).astype(BF16)

    @pl.when(i == 0)
    def _():
        lam = (jnp.exp(jnp.sum(lq1_ref[...] * lk1_ref[...], axis=-1, keepdims=True))
               - jnp.exp(jnp.sum(lq2_ref[...] * lk2_ref[...], axis=-1, keepdims=True))
               + lam_init)
        lam_ref[...] = jnp.broadcast_to(lam, lam_ref.shape)
        v0 = q0 + 2 * diff_width
        for c in range(0, diff_width, LANES):
            w_vt_ref[c:c + LANES, :] = jnp.transpose(
                w_in_ref[:, v0 + c:v0 + c + LANES]).astype(BF16)
        xn_ref[...] = normed(x0_ref)
        w_a_bf_ref[...] = w_a_ref[...].astype(BF16)
        for c in range(0, w_in_ref.shape[1], CAST_CHUNK):
            w_bf_ref[:, c:c + CAST_CHUNK] = w_in_ref[:, c:c + CAST_CHUNK].astype(BF16)

    @pl.when(i % tiles_per_seq == 0)
    def _():
        h_ref[0:SUBLANES, :] = jnp.zeros((SUBLANES, c0), F32)

    xn = xn_ref[...]

    uc = _dot(xn, w_bf_ref[:, 0:q0])
    h = uc[:, c0:2 * c0] * uc[:, 2 * c0:3 * c0]
    h_ref[SUBLANES:SUBLANES + tm, :] = h
    cw = conv_w_ref[...]
    conv = (cw[0:1, :] * h_ref[SUBLANES - 2:SUBLANES - 2 + tm, :]
            + cw[1:2, :] * h_ref[SUBLANES - 1:SUBLANES - 1 + tm, :]
            + cw[2:3, :] * h)
    h_ref[0:SUBLANES, :] = h[tm - SUBLANES:tm, :]
    gated = (uc[:, 0:c0] * conv).astype(BF16)

    gates = jax.nn.sigmoid(_dot(xn, w_bf_ref[:, g0:g0 + 2 * d_model]) + b_gate_ref[...])
    gb_ref[...] = gates[:, d_model:2 * d_model].astype(BF16)

    scale = DIFF_HEAD_DIM ** -0.5 * LOG2E
    q_ref[...] = (_dot(xn, w_bf_ref[:, q0:q0 + diff_width]) * scale).astype(BF16)
    k_ref[...] = _dot(xn, w_bf_ref[:, q0 + diff_width:q0 + 2 * diff_width]).astype(BF16)
    vt = _dot_nt(w_vt_ref[...], xn).astype(BF16)
    for hh in range(diff_width // DIFF_V_DIM):
        r0 = hh * VT_ROWS
        vt_ref[r0:r0 + DIFF_V_DIM, :] = vt[hh * DIFF_V_DIM:(hh + 1) * DIFF_V_DIM, :]
        vt_ref[r0 + DIFF_V_DIM:r0 + VT_ROWS, :] = jnp.ones((SUM_ROWS, tm), BF16)

    gaya_ref[...] = (gates[:, 0:d_model] * _dot(gated, w_a_bf_ref[...])).astype(BF16)

    xn_ref[...] = normed(x_next_ref)


def _in_proj(x2d, g, w_in, b_gate, conv_w, w_a, lq1, lk1, lq2, lk2, *, seq, lam_init):
    n, d_model = x2d.shape
    conv_width = conv_w.shape[1]
    diff_width = (w_in.shape[1] - 3 * conv_width - 2 * d_model) // 3
    tm = TOKEN_TILE
    assert seq % tm == 0 and n % seq == 0 and w_in.shape[1] % CAST_CHUNK == 0
    steps = n // tm
    vt_rows = (diff_width // DIFF_V_DIM) * VT_ROWS
    row = lambda i: (i, 0)
    kern = functools.partial(
        _in_proj_kernel, tiles_per_seq=seq // tm, conv_width=conv_width,
        diff_width=diff_width, d_model=d_model, lam_init=lam_init)
    return pl.pallas_call(
        kern,
        name="in_proj",
        grid=(steps,),
        in_specs=[
            pl.BlockSpec((tm, d_model), lambda i: (0, 0)),
            pl.BlockSpec((tm, d_model), lambda i: (jnp.minimum(i + 1, steps - 1), 0)),
            _resident(g.shape),
            _resident(w_in.shape),
            _resident(b_gate.shape),
            _resident(conv_w.shape),
            _resident(w_a.shape),
            _resident(lq1.shape), _resident(lk1.shape), _resident(lq2.shape), _resident(lk2.shape),
        ],
        out_specs=[
            pl.BlockSpec((tm, diff_width), row),
            pl.BlockSpec((tm, diff_width), row),
            pl.BlockSpec((vt_rows, tm), lambda i: (0, i)),
            pl.BlockSpec((tm, d_model), row),
            pl.BlockSpec((tm, d_model), row),
            pl.BlockSpec((SUBLANES, LANES), lambda i: (0, 0)),
        ],
        out_shape=[
            jax.ShapeDtypeStruct((n, diff_width), BF16),
            jax.ShapeDtypeStruct((n, diff_width), BF16),
            jax.ShapeDtypeStruct((vt_rows, n), BF16),
            jax.ShapeDtypeStruct((n, d_model), BF16),
            jax.ShapeDtypeStruct((n, d_model), BF16),
            jax.ShapeDtypeStruct((SUBLANES, LANES), F32),
        ],
        scratch_shapes=[
            pltpu.VMEM((SUBLANES + tm, conv_width), F32),
            pltpu.VMEM((diff_width, d_model), BF16),
            pltpu.VMEM((tm, d_model), BF16),
            pltpu.VMEM(w_a.shape, BF16),
            pltpu.VMEM(w_in.shape, BF16),
        ],
        compiler_params=pltpu.CompilerParams(
            dimension_semantics=("arbitrary",), vmem_limit_bytes=VMEM_LIMIT_BYTES),
    )(x2d, x2d, g, w_in, b_gate, conv_w, w_a, lq1, lk1, lq2, lk2)


def _alibi_features(seq, heads):
    pos = np.arange(seq)
    kfeat = np.zeros((seq, LANES), np.float32)
    cfeat = np.zeros((heads, LANES), np.float32)
    as_bf16 = lambda a: a.astype(BF16).astype(np.float32)
    slopes = np.array([2.0 ** (-8.0 * (h + 1) / heads) for h in range(heads)], np.float32)
    rest = (slopes * np.float32(LOG2E)).astype(np.float32)
    for piece in range(SLOPE_PIECES):
        part = as_bf16(rest)
        rest = (rest - part).astype(np.float32)
        kfeat[:, 2 * piece] = POS_SPLIT * (pos // POS_SPLIT)
        kfeat[:, 2 * piece + 1] = pos % POS_SPLIT
        cfeat[:, 2 * piece] = part
        cfeat[:, 2 * piece + 1] = part
    assert seq <= POS_SPLIT * 256 and np.array_equal(as_bf16(kfeat), kfeat)
    return jnp.asarray(kfeat, BF16), jnp.asarray(cfeat, F32)


def _attn_kernel(q_ref, k_ref, kfeat_ref, cfeat_ref, vt_ref, lam_ref, g_ref, o_ref,
                 q2_ref, mask_ref, sx_ref, sy_ref, sx_max_ref, sy_max_ref, m_ref, acc_ref,
                 *, out_scale):
    hg = pl.program_id(1)
    i = pl.program_id(2)
    last_tile = pl.num_programs(2) - 1
    t = Q_TILE
    kb = KEY_BLOCK
    d = DIFF_HEAD_DIM
    vd = DIFF_V_DIM
    hps = q2_ref.shape[0]
    sx = (sx_ref, sx_max_ref)
    sy = (sy_ref, sy_max_ref)
    nb = i // 2
    parity = i % 2

    def build_q2(tile):
        rows = pl.ds(pl.multiple_of(tile * t, t), t)
        lane = lax.broadcasted_iota(jnp.int32, (t, vd), 1)
        for hh in range(hps):
            q = q_ref[rows, hh * vd:(hh + 1) * vd]
            zero = jnp.zeros_like(q)
            q2_ref[hh, 0:t, 0:LANES] = jnp.where(lane < d, q, zero)
            q2_ref[hh, t:2 * t, 0:LANES] = jnp.where(lane >= d, q, zero)

    def block_start(b):
        return pl.multiple_of(b * kb, kb)

    def raw_scores(dst, b):
        dst_ref, dst_max_ref = dst
        start = block_start(b)
        feat = kfeat_ref[pl.ds(start, kb), :]
        for hh in range(hps):
            keys = jnp.concatenate([k_ref[pl.ds(start, kb), hh * vd:(hh + 1) * vd], feat], axis=1)
            s = _dot_nt(keys, q2_ref[hh])
            dst_ref[hh] = s
            dst_max_ref[hh] = jnp.max(s, axis=0, keepdims=True)

    def consume(src, b, mask=None, rows=kb):
        src_ref, src_max_ref = src
        start = block_start(b)
        for hh in range(hps):
            s = src_ref[hh, 0:rows, :]
            if mask is None:
                s_max = src_max_ref[hh]
            else:
                s = s + mask_ref[mask, 0:rows, :]
                s_max = jnp.max(s, axis=0, keepdims=True)
            m_old = m_ref[hh]
            m_new = jnp.maximum(m_old, s_max)
            alpha = jnp.exp2(m_old - m_new)
            p = jnp.exp2(s - m_new)
            m_ref[hh] = m_new
            pv = _dot(vt_ref[hh * VT_ROWS:(hh + 1) * VT_ROWS, pl.ds(start, rows)], p.astype(BF16))
            acc_ref[hh] = alpha * acc_ref[hh] + pv

    def consume_masked(src, b, diag_first):
        if diag_first:
            consume(src, b, mask=0, rows=t)
        else:
            consume(src, b, mask=1)

    @pl.when(i == 0)
    def _():
        key = lax.broadcasted_iota(jnp.int32, (kb, 2 * t), 0)
        qry = lax.broadcasted_iota(jnp.int32, (kb, 2 * t), 1)
        qry = jnp.where(qry >= t, qry - t, qry)
        zero = jnp.zeros((kb, 2 * t), F32)
        mask_ref[0] = jnp.where(key <= qry, zero, -jnp.inf)
        mask_ref[1] = jnp.where(key - t <= qry, zero, -jnp.inf)
        for hh in range(hps):
            row = cfeat_ref[pl.ds(hg * hps + hh, 1), :]
            q2_ref[hh, :, LANES:2 * LANES] = jnp.broadcast_to(row, (2 * t, LANES)).astype(BF16)
        build_q2(0)
        raw_scores(sx, 0)

    for hh in range(hps):
        m_ref[hh] = jnp.full((1, 2 * t), -jnp.inf, F32)
        acc_ref[hh] = jnp.zeros((VT_ROWS, 2 * t), F32)

    def body(pair, carry):
        b = 2 * pair
        raw_scores(sy, b + 1)
        consume(sx, b)
        raw_scores(sx, b + 2)
        consume(sy, b + 1)
        return carry

    lax.fori_loop(0, nb // 2, body, 0)

    for diag_first in (True, False):
        @pl.when((nb % 2 == 1) & (parity == (0 if diag_first else 1)))
        def _():
            raw_scores(sy, nb)
            consume(sx, nb - 1)
            consume_masked(sy, nb, diag_first)

        @pl.when((nb % 2 == 0) & (parity == (0 if diag_first else 1)))
        def _():
            consume_masked(sx, nb, diag_first)

    build_q2(jnp.minimum(i + 1, last_tile))
    raw_scores(sx, 0)

    lam = lam_ref[0:1, 0:1]
    for hh in range(hps):
        on = acc_ref[hh, 0:vd, :] * (1.0 / acc_ref[hh, vd:vd + 1, :])
        o = jnp.transpose(on[:, 0:t] - lam * on[:, t:2 * t])
        o = ((o * _rms_scale(o)) * g_ref[...]) * out_scale
        o_ref[:, hh * vd:(hh + 1) * vd] = o.astype(o_ref.dtype)


def _diff_attention(q, k, vt, lam, subln_g, *, batch, seq, out_scale):
    n, width = q.shape
    heads = width // DIFF_V_DIM
    hps = HEADS_PER_STEP
    t = Q_TILE
    assert seq % KEY_BLOCK == 0 and heads % hps == 0 and DIFF_V_DIM == LANES
    nq = seq // t
    kfeat, cfeat = _alibi_features(seq, heads)
    scores = pltpu.VMEM((hps, KEY_BLOCK, 2 * t), F32)
    per_head_seq = pl.BlockSpec((seq, hps * DIFF_V_DIM), lambda b, h, i: (b, h))
    return pl.pallas_call(
        functools.partial(_attn_kernel, out_scale=out_scale),
        name="diff_attn",
        grid=(batch, heads // hps, nq),
        in_specs=[
            per_head_seq,
            per_head_seq,
            _resident(kfeat.shape),
            _resident(cfeat.shape),
            pl.BlockSpec((hps * VT_ROWS, seq), lambda b, h, i: (h, b)),
            _resident(lam.shape),
            _resident(subln_g.shape),
        ],
        out_specs=pl.BlockSpec((t, hps * DIFF_V_DIM), lambda b, h, i: (b * nq + i, h)),
        out_shape=jax.ShapeDtypeStruct((n, width), BF16),
        scratch_shapes=[
            pltpu.VMEM((hps, 2 * t, 2 * LANES), BF16),
            pltpu.VMEM((2, KEY_BLOCK, 2 * t), F32),
            scores,
            scores,
            pltpu.VMEM((hps, 1, 2 * t), F32),
            pltpu.VMEM((hps, 1, 2 * t), F32),
            pltpu.VMEM((hps, 1, 2 * t), F32),
            pltpu.VMEM((hps, VT_ROWS, 2 * t), F32),
        ],
        compiler_params=pltpu.CompilerParams(
            dimension_semantics=("arbitrary", "arbitrary", "arbitrary"),
            vmem_limit_bytes=VMEM_LIMIT_BYTES),
    )(q, k, kfeat, cfeat, vt, lam, subln_g)


def _out_mlp_kernel(x_ref, gaya_ref, gb_ref, on_ref, w_b_ref, w_o_ref, g_mlp_ref,
                    w1_ref, w2_ref, g_fin_ref, out_ref, w_b_bf_ref, w_o_bf_ref, *, final_norm):
    @pl.when(pl.program_id(0) == 0)
    def _():
        w_b_bf_ref[...] = w_b_ref[...].astype(BF16)
        w_o_bf_ref[...] = w_o_ref[...].astype(BF16)

    y_b = _dot(on_ref[...], w_b_bf_ref[...])
    merged = gaya_ref[...].astype(F32) + gb_ref[...].astype(F32) * y_b
    x1 = x_ref[...] + _dot(merged.astype(BF16), w_o_bf_ref[...])
    xn = ((x1 * _rms_scale(x1)) * g_mlp_ref[...]).astype(BF16)
    d_ff = w1_ref.shape[1]
    acc = x1
    for c in range(0, d_ff, FF_CHUNK):
        hc = jnp.maximum(_dot(xn, w1_ref[:, c:c + FF_CHUNK]), 0.0)
        acc = acc + _dot((hc * hc).astype(BF16), w2_ref[c:c + FF_CHUNK, :])
    if final_norm:
        acc = (acc * _rms_scale(acc)) * g_fin_ref[...]
    out_ref[...] = acc


def _out_mlp(x2d, gaya, gb, on, w_b, w_o, g_mlp, w1, w2, g_fin, *, final_norm):
    n, d_model = x2d.shape
    tm = TOKEN_TILE
    assert n % tm == 0 and w1.shape[1] % FF_CHUNK == 0
    row = lambda i: (i, 0)
    return pl.pallas_call(
        functools.partial(_out_mlp_kernel, final_norm=final_norm),
        name="out_mlp",
        grid=(n // tm,),
        in_specs=[
            pl.BlockSpec((tm, d_model), row),
            pl.BlockSpec((tm, d_model), row),
            pl.BlockSpec((tm, d_model), row),
            pl.BlockSpec((tm, on.shape[1]), row),
            _resident(w_b.shape),
            _resident(w_o.shape),
            _resident(g_mlp.shape),
            _resident(w1.shape),
            _resident(w2.shape),
            _resident(g_fin.shape),
        ],
        out_specs=pl.BlockSpec((tm, d_model), row),
        out_shape=jax.ShapeDtypeStruct((n, d_model), F32),
        scratch_shapes=[
            pltpu.VMEM(w_b.shape, BF16),
            pltpu.VMEM(w_o.shape, BF16),
        ],
        compiler_params=pltpu.CompilerParams(
            dimension_semantics=("arbitrary",), vmem_limit_bytes=VMEM_LIMIT_BYTES),
    )(x2d, gaya, gb, on, w_b, w_o, g_mlp, w1, w2, g_fin)


def _lambda_init(layer_idx):
    return 0.8 - 0.6 * math.exp(-0.3 * layer_idx)


def kernel(x, norm_mix_g, w_in, b_gate, conv_w, lambda_q1, lambda_k1, lambda_q2, lambda_k2,
           subln_g, w_a_out, w_b_out, w_o, norm_mlp_g, w_mlp_in, w_mlp_out, norm_final_g):
    batch, seq, d_model = x.shape
    depth = w_in.shape[0]
    x2d = x.reshape(batch * seq, d_model)
    row = lambda a: a.reshape(1, -1)
    for l in range(depth):
        lam_init = _lambda_init(l)
        q, k, vt, gaya, gb, lam = _in_proj(
            x2d, row(norm_mix_g[l]), w_in[l], row(b_gate[l]), conv_w[l],
            w_a_out[l], row(lambda_q1[l]), row(lambda_k1[l]),
            row(lambda_q2[l]), row(lambda_k2[l]), seq=seq, lam_init=lam_init)
        on = _diff_attention(q, k, vt, lam, row(subln_g[l]), batch=batch, seq=seq,
                             out_scale=1.0 - lam_init)
        x2d = _out_mlp(
            x2d, gaya, gb, on, w_b_out[l], w_o[l],
            row(norm_mlp_g[l]), w_mlp_in[l].astype(BF16), w_mlp_out[l].astype(BF16),
            row(norm_final_g), final_norm=(l == depth - 1))
    return x2d.reshape(batch, seq, d_model)
```

```python
import functools
import math

import jax
import jax.numpy as jnp
import numpy as np
from jax import lax
from jax.experimental import pallas as pl
from jax.experimental.pallas import tpu as pltpu

F32 = jnp.float32
BF16 = jnp.bfloat16

RMS_EPS = 1e-6
CONV_K = 3
DIFF_HEAD_DIM = 64
DIFF_V_DIM = 2 * DIFF_HEAD_DIM

LOG2E = math.log2(math.e)
LANES = 128
SUM_ROWS = 16
VT_ROWS = DIFF_V_DIM + SUM_ROWS
SUBLANES = 8
VMEM_LIMIT_BYTES = 56 * 1024 * 1024

TOKEN_TILE = 512
FF_CHUNK = 1024
CAST_CHUNK = 512
HEADS_PER_STEP = 4
Q_TILE = 256
KEY_BLOCK = 2 * Q_TILE
POS_SPLIT = 16
SLOPE_PIECES = 3


def _rms_scale(x):
    return lax.rsqrt(jnp.mean(x * x, axis=-1, keepdims=True) + RMS_EPS)


def _dot(a, b):
    return jnp.dot(a, b, preferred_element_type=F32)


def _dot_nt(a, b):
    return lax.dot_general(a, b, (((1,), (1,)), ((), ())), preferred_element_type=F32)


def _resident(shape):
    return pl.BlockSpec(shape, lambda *_: (0,) * len(shape), pipeline_mode=pl.Buffered(1))


def _in_proj_kernel(x0_ref, x_next_ref, g_ref, w_in_ref, b_gate_ref, conv_w_ref, w_a_ref,
                    lq1_ref, lk1_ref, lq2_ref, lk2_ref,
                    q_ref, k_ref, vt_ref, gaya_ref, gb_ref, lam_ref,
                    h_ref, w_vt_ref, xn_ref, w_a_bf_ref, w_bf_ref, *, tiles_per_seq, conv_width,
                    diff_width, d_model, lam_init):
    i = pl.program_id(0)
    tm = xn_ref.shape[0]
    c0 = conv_width
    q0 = 3 * conv_width
    g0 = q0 + 3 * diff_width

    def normed(x_ref):
        x = x_ref[...]
        return ((x * _rms_scale(x)) * g_ref[...]).astype(BF16)

    @pl.when(i == 0)
    def _():
        lam = (jnp.exp(jnp.sum(lq1_ref[...] * lk1_ref[...], axis=-1, keepdims=True))
               - jnp.exp(jnp.sum(lq2_ref[...] * lk2_ref[...], axis=-1, keepdims=True))
               + lam_init)
        lam_ref[...] = jnp.broadcast_to(lam, lam_ref.shape)
        v0 = q0 + 2 * diff_width
        for c in range(0, diff_width, LANES):
            w_vt_ref[c:c + LANES, :] = jnp.transpose(
                w_in_ref[:, v0 + c:v0 + c + LANES]).astype(BF16)
        xn_ref[...] = normed(x0_ref)
        w_a_bf_ref[...] = w_a_ref[...].astype(BF16)
        for c in range(0, w_in_ref.shape[1], CAST_CHUNK):
            w_bf_ref[:, c:c + CAST_CHUNK] = w_in_ref[:, c:c + CAST_CHUNK].astype(BF16)

    @pl.when(i % tiles_per_seq == 0)
    def _():
        h_ref[0:SUBLANES, :] = jnp.zeros((SUBLANES, c0), F32)

    xn = xn_ref[...]

    uc = _dot(xn, w_bf_ref[:, 0:q0])
    h = uc[:, c0:2 * c0] * uc[:, 2 * c0:3 * c0]
    h_ref[SUBLANES:SUBLANES + tm, :] = h
    cw = conv_w_ref[...]
    conv = (cw[0:1, :] * h_ref[SUBLANES - 2:SUBLANES - 2 + tm, :]
            + cw[1:2, :] * h_ref[SUBLANES - 1:SUBLANES - 1 + tm, :]
            + cw[2:3, :] * h)
    h_ref[0:SUBLANES, :] = h[tm - SUBLANES:tm, :]
    gated = (uc[:, 0:c0] * conv).astype(BF16)

    gates = jax.nn.sigmoid(_dot(xn, w_bf_ref[:, g0:g0 + 2 * d_model]) + b_gate_ref[...])
    gb_ref[...] = gates[:, d_model:2 * d_model].astype(BF16)

    scale = DIFF_HEAD_DIM ** -0.5 * LOG2E
    q_ref[...] = (_dot(xn, w_bf_ref[:, q0:q0 + diff_width]) * scale).astype(BF16)
    k_ref[...] = _dot(xn, w_bf_ref[:, q0 + diff_width:q0 + 2 * diff_width]).astype(BF16)
    vt = _dot_nt(w_vt_ref[...], xn).astype(BF16)
    for hh in range(diff_width // DIFF_V_DIM):
        r0 = hh * VT_ROWS
        vt_ref[r0:r0 + DIFF_V_DIM, :] = vt[hh * DIFF_V_DIM:(hh + 1) * DIFF_V_DIM, :]
        vt_ref[r0 + DIFF_V_DIM:r0 + VT_ROWS, :] = jnp.ones((SUM_ROWS, tm), BF16)

    gaya_ref[...] = (gates[:, 0:d_model] * _dot(gated, w_a_bf_ref[...])).astype(BF16)

    xn_ref[...] = normed(x_next_ref)


def _in_proj(x2d, g, w_in, b_gate, conv_w, w_a, lq1, lk1, lq2, lk2, *, seq, lam_init):
    n, d_model = x2d.shape
    conv_width = conv_w.shape[1]
    diff_width = (w_in.shape[1] - 3 * conv_width - 2 * d_model) // 3
    tm = TOKEN_TILE
    assert seq % tm == 0 and n % seq == 0 and w_in.shape[1] % CAST_CHUNK == 0
    steps = n // tm
    vt_rows = (diff_width // DIFF_V_DIM) * VT_ROWS
    row = lambda i: (i, 0)
    kern = functools.partial(
        _in_proj_kernel, tiles_per_seq=seq // tm, conv_width=conv_width,
        diff_width=diff_width, d_model=d_model, lam_init=lam_init)
    return pl.pallas_call(
        kern,
        name="in_proj",
        grid=(steps,),
        in_specs=[
            pl.BlockSpec((tm, d_model), lambda i: (0, 0)),
            pl.BlockSpec((tm, d_model), lambda i: (jnp.minimum(i + 1, steps - 1), 0)),
            _resident(g.shape),
            _resident(w_in.shape),
            _resident(b_gate.shape),
            _resident(conv_w.shape),
            _resident(w_a.shape),
            _resident(lq1.shape), _resident(lk1.shape), _resident(lq2.shape), _resident(lk2.shape),
        ],
        out_specs=[
            pl.BlockSpec((tm, diff_width), row),
            pl.BlockSpec((tm, diff_width), row),
            pl.BlockSpec((vt_rows, tm), lambda i: (0, i)),
            pl.BlockSpec((tm, d_model), row),
            pl.BlockSpec((tm, d_model), row),
            pl.BlockSpec((SUBLANES, LANES), lambda i: (0, 0)),
        ],
        out_shape=[
            jax.ShapeDtypeStruct((n, diff_width), BF16),
            jax.ShapeDtypeStruct((n, diff_width), BF16),
            jax.ShapeDtypeStruct((vt_rows, n), BF16),
            jax.ShapeDtypeStruct((n, d_model), BF16),
            jax.ShapeDtypeStruct((n, d_model), BF16),
            jax.ShapeDtypeStruct((SUBLANES, LANES), F32),
        ],
        scratch_shapes=[
            pltpu.VMEM((SUBLANES + tm, conv_width), F32),
            pltpu.VMEM((diff_width, d_model), BF16),
            pltpu.VMEM((tm, d_model), BF16),
            pltpu.VMEM(w_a.shape, BF16),
            pltpu.VMEM(w_in.shape, BF16),
        ],
        compiler_params=pltpu.CompilerParams(
            dimension_semantics=("arbitrary",), vmem_limit_bytes=VMEM_LIMIT_BYTES),
    )(x2d, x2d, g, w_in, b_gate, conv_w, w_a, lq1, lk1, lq2, lk2)


def _alibi_features(seq, heads):
    pos = np.arange(seq)
    kfeat = np.zeros((seq, LANES), np.float32)
    cfeat = np.zeros((heads, LANES), np.float32)
    as_bf16 = lambda a: a.astype(BF16).astype(np.float32)
    slopes = np.array([2.0 ** (-8.0 * (h + 1) / heads) for h in range(heads)], np.float32)
    rest = (slopes * np.float32(LOG2E)).astype(np.float32)
    for piece in range(SLOPE_PIECES):
        part = as_bf16(rest)
        rest = (rest - part).astype(np.float32)
        kfeat[:, 2 * piece] = POS_SPLIT * (pos // POS_SPLIT)
        kfeat[:, 2 * piece + 1] = pos % POS_SPLIT
        cfeat[:, 2 * piece] = part
        cfeat[:, 2 * piece + 1] = part
    assert seq <= POS_SPLIT * 256 and np.array_equal(as_bf16(kfeat), kfeat)
    return jnp.asarray(kfeat, BF16), jnp.asarray(cfeat, F32)


def _attn_kernel(q_ref, k_ref, kfeat_ref, cfeat_ref, vt_ref, lam_ref, g_ref, o_ref,
                 q2_ref, mask_ref, sx_ref, sy_ref, sx_max_ref, sy_max_ref, m_ref, acc_ref,
                 *, out_scale):
    hg = pl.program_id(1)
    i = pl.program_id(2)
    last_tile = pl.num_programs(2) - 1
    t = Q_TILE
    kb = KEY_BLOCK
    d = DIFF_HEAD_DIM
    vd = DIFF_V_DIM
    hps = q2_ref.shape[0]
    sx = (sx_ref, sx_max_ref)
    sy = (sy_ref, sy_max_ref)
    nb = i // 2
    parity = i % 2

    def build_q2(tile):
        rows = pl.ds(pl.multiple_of(tile * t, t), t)
        lane = lax.broadcasted_iota(jnp.int32, (t, vd), 1)
        for hh in range(hps):
            q = q_ref[rows, hh * vd:(hh + 1) * vd]
            zero = jnp.zeros_like(q)
            q2_ref[hh, 0:t, 0:LANES] = jnp.where(lane < d, q, zero)
            q2_ref[hh, t:2 * t, 0:LANES] = jnp.where(lane >= d, q, zero)

    def block_start(b):
        return pl.multiple_of(b * kb, kb)

    def raw_scores(dst, b):
        dst_ref, dst_max_ref = dst
        start = block_start(b)
        feat = kfeat_ref[pl.ds(start, kb), :]
        for hh in range(hps):
            keys = jnp.concatenate([k_ref[pl.ds(start, kb), hh * vd:(hh + 1) * vd], feat], axis=1)
            s = _dot_nt(keys, q2_ref[hh])
            dst_ref[hh] = s
            dst_max_ref[hh] = jnp.max(s, axis=0, keepdims=True)

    def consume(src, b, mask=None, rows=kb):
        src_ref, src_max_ref = src
        start = block_start(b)
        for hh in range(hps):
            s = src_ref[hh, 0:rows, :]
            if mask is None:
                s_max = src_max_ref[hh]
            else:
                s = s + mask_ref[mask, 0:rows, :]
                s_max = jnp.max(s, axis=0, keepdims=True)
            m_old = m_ref[hh]
            m_new = jnp.maximum(m_old, s_max)
            alpha = jnp.exp2(m_old - m_new)
            p = jnp.exp2(s - m_new)
            m_ref[hh] = m_new
            pv = _dot(vt_ref[hh * VT_ROWS:(hh + 1) * VT_ROWS, pl.ds(start, rows)], p.astype(BF16))
            acc_ref[hh] = alpha * acc_ref[hh] + pv

    def consume_masked(src, b, diag_first):
        if diag_first:
            consume(src, b, mask=0, rows=t)
        else:
            consume(src, b, mask=1)

    @pl.when(i == 0)
    def _():
        key = lax.broadcasted_iota(jnp.int32, (kb, 2 * t), 0)
        qry = lax.broadcasted_iota(jnp.int32, (kb, 2 * t), 1)
        qry = jnp.where(qry >= t, qry - t, qry)
        zero = jnp.zeros((kb, 2 * t), F32)
        mask_ref[0] = jnp.where(key <= qry, zero, -jnp.inf)
        mask_ref[1] = jnp.where(key - t <= qry, zero, -jnp.inf)
        for hh in range(hps):
            row = cfeat_ref[pl.ds(hg * hps + hh, 1), :]
            q2_ref[hh, :, LANES:2 * LANES] = jnp.broadcast_to(row, (2 * t, LANES)).astype(BF16)
        build_q2(0)
        raw_scores(sx, 0)

    for hh in range(hps):
        m_ref[hh] = jnp.full((1, 2 * t), -jnp.inf, F32)
        acc_ref[hh] = jnp.zeros((VT_ROWS, 2 * t), F32)

    def body(pair, carry):
        b = 2 * pair
        raw_scores(sy, b + 1)
        consume(sx, b)
        raw_scores(sx, b + 2)
        consume(sy, b + 1)
        return carry

    lax.fori_loop(0, nb // 2, body, 0)

    def finish():
        build_q2(jnp.minimum(i + 1, last_tile))
        raw_scores(sx, 0)
        lam = lam_ref[0:1, 0:1]
        for hh in range(hps):
            on = acc_ref[hh, 0:vd, :] * (1.0 / acc_ref[hh, vd:vd + 1, :])
            o = jnp.transpose(on[:, 0:t] - lam * on[:, t:2 * t])
            o = ((o * _rms_scale(o)) * g_ref[...]) * out_scale
            o_ref[:, hh * vd:(hh + 1) * vd] = o.astype(o_ref.dtype)

    for diag_first in (True, False):
        @pl.when((nb % 2 == 1) & (parity == (0 if diag_first else 1)))
        def _():
            raw_scores(sy, nb)
            consume(sx, nb - 1)
            consume_masked(sy, nb, diag_first)

        @pl.when((nb % 2 == 0) & (parity == (0 if diag_first else 1)))
        def _():
            consume_masked(sx, nb, diag_first)
            finish()

    @pl.when(nb % 2 == 1)
    def _():
        finish()


def _diff_attention(q, k, vt, lam, subln_g, *, batch, seq, out_scale):
    n, width = q.shape
    heads = width // DIFF_V_DIM
    hps = HEADS_PER_STEP
    t = Q_TILE
    assert seq % KEY_BLOCK == 0 and heads % hps == 0 and DIFF_V_DIM == LANES
    nq = seq // t
    kfeat, cfeat = _alibi_features(seq, heads)
    scores = pltpu.VMEM((hps, KEY_BLOCK, 2 * t), F32)
    per_head_seq = pl.BlockSpec((seq, hps * DIFF_V_DIM), lambda b, h, i: (b, h))
    return pl.pallas_call(
        functools.partial(_attn_kernel, out_scale=out_scale),
        name="diff_attn",
        grid=(batch, heads // hps, nq),
        in_specs=[
            per_head_seq,
            per_head_seq,
            _resident(kfeat.shape),
            _resident(cfeat.shape),
            pl.BlockSpec((hps * VT_ROWS, seq), lambda b, h, i: (h, b)),
            _resident(lam.shape),
            _resident(subln_g.shape),
        ],
        out_specs=pl.BlockSpec((t, hps * DIFF_V_DIM), lambda b, h, i: (b * nq + i, h)),
        out_shape=jax.ShapeDtypeStruct((n, width), BF16),
        scratch_shapes=[
            pltpu.VMEM((hps, 2 * t, 2 * LANES), BF16),
            pltpu.VMEM((2, KEY_BLOCK, 2 * t), F32),
            scores,
            scores,
            pltpu.VMEM((hps, 1, 2 * t), F32),
            pltpu.VMEM((hps, 1, 2 * t), F32),
            pltpu.VMEM((hps, 1, 2 * t), F32),
            pltpu.VMEM((hps, VT_ROWS, 2 * t), F32),
        ],
        compiler_params=pltpu.CompilerParams(
            dimension_semantics=("arbitrary", "arbitrary", "arbitrary"),
            vmem_limit_bytes=VMEM_LIMIT_BYTES),
    )(q, k, kfeat, cfeat, vt, lam, subln_g)


def _out_mlp_kernel(x_ref, gaya_ref, gb_ref, on_ref, w_b_ref, w_o_ref, g_mlp_ref,
                    w1_ref, w2_ref, g_fin_ref, out_ref, w_b_bf_ref, w_o_bf_ref, *, final_norm):
    @pl.when(pl.program_id(0) == 0)
    def _():
        w_b_bf_ref[...] = w_b_ref[...].astype(BF16)
        w_o_bf_ref[...] = w_o_ref[...].astype(BF16)

    y_b = _dot(on_ref[...], w_b_bf_ref[...])
    merged = gaya_ref[...].astype(F32) + gb_ref[...].astype(F32) * y_b
    x1 = x_ref[...] + _dot(merged.astype(BF16), w_o_bf_ref[...])
    xn = ((x1 * _rms_scale(x1)) * g_mlp_ref[...]).astype(BF16)
    d_ff = w1_ref.shape[1]
    acc = x1
    for c in range(0, d_ff, FF_CHUNK):
        hc = jnp.maximum(_dot(xn, w1_ref[:, c:c + FF_CHUNK]), 0.0)
        acc = acc + _dot((hc * hc).astype(BF16), w2_ref[c:c + FF_CHUNK, :])
    if final_norm:
        acc = (acc * _rms_scale(acc)) * g_fin_ref[...]
    out_ref[...] = acc


def _out_mlp(x2d, gaya, gb, on, w_b, w_o, g_mlp, w1, w2, g_fin, *, final_norm):
    n, d_model = x2d.shape
    tm = TOKEN_TILE
    assert n % tm == 0 and w1.shape[1] % FF_CHUNK == 0
    row = lambda i: (i, 0)
    return pl.pallas_call(
        functools.partial(_out_mlp_kernel, final_norm=final_norm),
        name="out_mlp",
        grid=(n // tm,),
        in_specs=[
            pl.BlockSpec((tm, d_model), row),
            pl.BlockSpec((tm, d_model), row),
            pl.BlockSpec((tm, d_model), row),
            pl.BlockSpec((tm, on.shape[1]), row),
            _resident(w_b.shape),
            _resident(w_o.shape),
            _resident(g_mlp.shape),
            _resident(w1.shape),
            _resident(w2.shape),
            _resident(g_fin.shape),
        ],
        out_specs=pl.BlockSpec((tm, d_model), row),
        out_shape=jax.ShapeDtypeStruct((n, d_model), F32),
        scratch_shapes=[
            pltpu.VMEM(w_b.shape, BF16),
            pltpu.VMEM(w_o.shape, BF16),
        ],
        compiler_params=pltpu.CompilerParams(
            dimension_semantics=("arbitrary",), vmem_limit_bytes=VMEM_LIMIT_BYTES),
    )(x2d, gaya, gb, on, w_b, w_o, g_mlp, w1, w2, g_fin)


def _lambda_init(layer_idx):
    return 0.8 - 0.6 * math.exp(-0.3 * layer_idx)


def kernel(x, norm_mix_g, w_in, b_gate, conv_w, lambda_q1, lambda_k1, lambda_q2, lambda_k2,
           subln_g, w_a_out, w_b_out, w_o, norm_mlp_g, w_mlp_in, w_mlp_out, norm_final_g):
    batch, seq, d_model = x.shape
    depth = w_in.shape[0]
    x2d = x.reshape(batch * seq, d_model)
    row = lambda a: a.reshape(1, -1)
    for l in range(depth):
        lam_init = _lambda_init(l)
        q, k, vt, gaya, gb, lam = _in_proj(
            x2d, row(norm_mix_g[l]), w_in[l], row(b_gate[l]), conv_w[l],
            w_a_out[l], row(lambda_q1[l]), row(lambda_k1[l]),
            row(lambda_q2[l]), row(lambda_k2[l]), seq=seq, lam_init=lam_init)
        on = _diff_attention(q, k, vt, lam, row(subln_g[l]), batch=batch, seq=seq,
                             out_scale=1.0 - lam_init)
        x2d = _out_mlp(
            x2d, gaya, gb, on, w_b_out[l], w_o[l],
            row(norm_mlp_g[l]), w_mlp_in[l].astype(BF16), w_mlp_out[l].astype(BF16),
            row(norm_final_g), final_norm=(l == depth - 1))
    return x2d.reshape(batch, seq, d_model)
```

```python
import functools
import math

import jax
import jax.numpy as jnp
import numpy as np
from jax import lax
from jax.experimental import pallas as pl
from jax.experimental.pallas import tpu as pltpu

F32 = jnp.float32
BF16 = jnp.bfloat16

RMS_EPS = 1e-6
CONV_K = 3
DIFF_HEAD_DIM = 64
DIFF_V_DIM = 2 * DIFF_HEAD_DIM

LOG2E = math.log2(math.e)
LANES = 128
SUM_ROWS = 16
VT_ROWS = DIFF_V_DIM + SUM_ROWS
SUBLANES = 8
VMEM_LIMIT_BYTES = 56 * 1024 * 1024

TOKEN_TILE = 512
FF_CHUNK = 1024
CAST_CHUNK = 512
HEADS_PER_STEP = 4
Q_TILE = 256
KEY_BLOCK = 2 * Q_TILE
POS_SPLIT = 16
SLOPE_PIECES = 3


def _rms_scale(x):
    return lax.rsqrt(jnp.mean(x * x, axis=-1, keepdims=True) + RMS_EPS)


def _dot(a, b):
    return jnp.dot(a, b, preferred_element_type=F32)


def _dot_nt(a, b):
    return lax.dot_general(a, b, (((1,), (1,)), ((), ())), preferred_element_type=F32)


def _resident(shape):
    return pl.BlockSpec(shape, lambda *_: (0,) * len(shape), pipeline_mode=pl.Buffered(1))


def _in_proj_kernel(x0_ref, x_next_ref, g_ref, w_in_ref, b_gate_ref, conv_w_ref, w_a_ref,
                    lq1_ref, lk1_ref, lq2_ref, lk2_ref,
                    q_ref, k_ref, vt_ref, gaya_ref, gb_ref, lam_ref,
                    h_ref, w_vt_ref, xn_ref, w_a_bf_ref, w_bf_ref, *, tiles_per_seq, conv_width,
                    diff_width, d_model, lam_init):
    i = pl.program_id(0)
    tm = xn_ref.shape[0]
    c0 = conv_width
    q0 = 3 * conv_width
    g0 = q0 + 3 * diff_width

    def normed(x_ref):
        x = x_ref[...]
        return ((x * _rms_scale(x)) * g_ref[...]).astype(BF16)

    @pl.when(i == 0)
    def _():
        lam = (jnp.exp(jnp.sum(lq1_ref[...] * lk1_ref[...], axis=-1, keepdims=True))
               - jnp.exp(jnp.sum(lq2_ref[...] * lk2_ref[...], axis=-1, keepdims=True))
               + lam_init)
        lam_ref[...] = jnp.broadcast_to(lam, lam_ref.shape)
        v0 = q0 + 2 * diff_width
        for c in range(0, diff_width, LANES):
            w_vt_ref[c:c + LANES, :] = jnp.transpose(
                w_in_ref[:, v0 + c:v0 + c + LANES]).astype(BF16)
        xn_ref[...] = normed(x0_ref)
        w_a_bf_ref[...] = w_a_ref[...].astype(BF16)
        for c in range(0, w_in_ref.shape[1], CAST_CHUNK):
            w_bf_ref[:, c:c + CAST_CHUNK] = w_in_ref[:, c:c + CAST_CHUNK].astype(BF16)

    @pl.when(i % tiles_per_seq == 0)
    def _():
        h_ref[0:SUBLANES, :] = jnp.zeros((SUBLANES, c0), F32)

    xn = xn_ref[...]

    uc = _dot(xn, w_bf_ref[:, 0:q0])
    h = uc[:, c0:2 * c0] * uc[:, 2 * c0:3 * c0]
    h_ref[SUBLANES:SUBLANES + tm, :] = h
    cw = conv_w_ref[...]
    conv = (cw[0:1, :] * h_ref[SUBLANES - 2:SUBLANES - 2 + tm, :]
            + cw[1:2, :] * h_ref[SUBLANES - 1:SUBLANES - 1 + tm, :]
            + cw[2:3, :] * h)
    h_ref[0:SUBLANES, :] = h[tm - SUBLANES:tm, :]
    gated = (uc[:, 0:c0] * conv).astype(BF16)

    gates = jax.nn.sigmoid(_dot(xn, w_bf_ref[:, g0:g0 + 2 * d_model]) + b_gate_ref[...])
    gb_ref[...] = gates[:, d_model:2 * d_model].astype(BF16)

    scale = DIFF_HEAD_DIM ** -0.5 * LOG2E
    q_ref[...] = (_dot(xn, w_bf_ref[:, q0:q0 + diff_width]) * scale).astype(BF16)
    k_ref[...] = _dot(xn, w_bf_ref[:, q0 + diff_width:q0 + 2 * diff_width]).astype(BF16)
    vt = _dot_nt(w_vt_ref[...], xn).astype(BF16)
    for hh in range(diff_width // DIFF_V_DIM):
        r0 = hh * VT_ROWS
        vt_ref[r0:r0 + DIFF_V_DIM, :] = vt[hh * DIFF_V_DIM:(hh + 1) * DIFF_V_DIM, :]
        vt_ref[r0 + DIFF_V_DIM:r0 + VT_ROWS, :] = jnp.ones((SUM_ROWS, tm), BF16)

    gaya_ref[...] = (gates[:, 0:d_model] * _dot(gated, w_a_bf_ref[...])).astype(BF16)

    xn_ref[...] = normed(x_next_ref)


def _in_proj(x2d, g, w_in, b_gate, conv_w, w_a, lq1, lk1, lq2, lk2, *, seq, lam_init):
    n, d_model = x2d.shape
    conv_width = conv_w.shape[1]
    diff_width = (w_in.shape[1] - 3 * conv_width - 2 * d_model) // 3
    tm = TOKEN_TILE
    assert seq % tm == 0 and n % seq == 0 and w_in.shape[1] % CAST_CHUNK == 0
    steps = n // tm
    vt_rows = (diff_width // DIFF_V_DIM) * VT_ROWS
    row = lambda i: (i, 0)
    kern = functools.partial(
        _in_proj_kernel, tiles_per_seq=seq // tm, conv_width=conv_width,
        diff_width=diff_width, d_model=d_model, lam_init=lam_init)
    return pl.pallas_call(
        kern,
        name="in_proj",
        grid=(steps,),
        in_specs=[
            pl.BlockSpec((tm, d_model), lambda i: (0, 0)),
            pl.BlockSpec((tm, d_model), lambda i: (jnp.minimum(i + 1, steps - 1), 0)),
            _resident(g.shape),
            _resident(w_in.shape),
            _resident(b_gate.shape),
            _resident(conv_w.shape),
            _resident(w_a.shape),
            _resident(lq1.shape), _resident(lk1.shape), _resident(lq2.shape), _resident(lk2.shape),
        ],
        out_specs=[
            pl.BlockSpec((tm, diff_width), row),
            pl.BlockSpec((tm, diff_width), row),
            pl.BlockSpec((vt_rows, tm), lambda i: (0, i)),
            pl.BlockSpec((tm, d_model), row),
            pl.BlockSpec((tm, d_model), row),
            pl.BlockSpec((SUBLANES, LANES), lambda i: (0, 0)),
        ],
        out_shape=[
            jax.ShapeDtypeStruct((n, diff_width), BF16),
            jax.ShapeDtypeStruct((n, diff_width), BF16),
            jax.ShapeDtypeStruct((vt_rows, n), BF16),
            jax.ShapeDtypeStruct((n, d_model), BF16),
            jax.ShapeDtypeStruct((n, d_model), BF16),
            jax.ShapeDtypeStruct((SUBLANES, LANES), F32),
        ],
        scratch_shapes=[
            pltpu.VMEM((SUBLANES + tm, conv_width), F32),
            pltpu.VMEM((diff_width, d_model), BF16),
            pltpu.VMEM((tm, d_model), BF16),
            pltpu.VMEM(w_a.shape, BF16),
            pltpu.VMEM(w_in.shape, BF16),
        ],
        compiler_params=pltpu.CompilerParams(
            dimension_semantics=("arbitrary",), vmem_limit_bytes=VMEM_LIMIT_BYTES),
    )(x2d, x2d, g, w_in, b_gate, conv_w, w_a, lq1, lk1, lq2, lk2)


def _alibi_features(seq, heads):
    pos = np.arange(seq)
    kfeat = np.zeros((seq, LANES), np.float32)
    cfeat = np.zeros((heads, LANES), np.float32)
    as_bf16 = lambda a: a.astype(BF16).astype(np.float32)
    slopes = np.array([2.0 ** (-8.0 * (h + 1) / heads) for h in range(heads)], np.float32)
    rest = (slopes * np.float32(LOG2E)).astype(np.float32)
    for piece in range(SLOPE_PIECES):
        part = as_bf16(rest)
        rest = (rest - part).astype(np.float32)
        kfeat[:, 2 * piece] = POS_SPLIT * (pos // POS_SPLIT)
        kfeat[:, 2 * piece + 1] = pos % POS_SPLIT
        cfeat[:, 2 * piece] = part
        cfeat[:, 2 * piece + 1] = part
    assert seq <= POS_SPLIT * 256 and np.array_equal(as_bf16(kfeat), kfeat)
    return jnp.asarray(kfeat, BF16), jnp.asarray(cfeat, F32)


def _attn_kernel(q_ref, k_ref, kfeat_ref, cfeat_ref, vt_ref, lam_ref, g_ref, o_ref,
                 q2_ref, mask_ref, sx_ref, sy_ref, sx_max_ref, sy_max_ref, m_ref, acc_ref,
                 *, out_scale):
    hg = pl.program_id(1)
    i = pl.program_id(2)
    last_tile = pl.num_programs(2) - 1
    t = Q_TILE
    kb = KEY_BLOCK
    d = DIFF_HEAD_DIM
    vd = DIFF_V_DIM
    hps = q2_ref.shape[0]
    sx = (sx_ref, sx_max_ref)
    sy = (sy_ref, sy_max_ref)
    nb = i // 2
    parity = i % 2

    def build_q2(tile):
        rows = pl.ds(pl.multiple_of(tile * t, t), t)
        lane = lax.broadcasted_iota(jnp.int32, (t, vd), 1)
        for hh in range(hps):
            q = q_ref[rows, hh * vd:(hh + 1) * vd]
            zero = jnp.zeros_like(q)
            q2_ref[hh, 0:t, 0:LANES] = jnp.where(lane < d, q, zero)
            q2_ref[hh, t:2 * t, 0:LANES] = jnp.where(lane >= d, q, zero)

    def block_start(b):
        return pl.multiple_of(b * kb, kb)

    def raw_scores(dst, b):
        dst_ref, dst_max_ref = dst
        start = block_start(b)
        feat = kfeat_ref[pl.ds(start, kb), :]
        for hh in range(hps):
            keys = jnp.concatenate([k_ref[pl.ds(start, kb), hh * vd:(hh + 1) * vd], feat], axis=1)
            s = _dot_nt(keys, q2_ref[hh])
            dst_ref[hh, :, 0:2 * t] = s
            dst_max_ref[hh] = jnp.max(s, axis=0, keepdims=True)

    def consume(src, b, mask=None, rows=kb):
        src_ref, src_max_ref = src
        start = block_start(b)
        for hh in range(hps):
            s = src_ref[hh, 0:rows, 0:2 * t]
            if mask is None:
                s_max = src_max_ref[hh]
            else:
                s = s + mask_ref[mask, 0:rows, :]
                s_max = jnp.max(s, axis=0, keepdims=True)
            m_old = m_ref[hh]
            m_new = jnp.maximum(m_old, s_max)
            alpha = jnp.exp2(m_old - m_new)
            p = jnp.exp2(s - m_new)
            m_ref[hh] = m_new
            pv = _dot(vt_ref[hh * VT_ROWS:(hh + 1) * VT_ROWS, pl.ds(start, rows)], p.astype(BF16))
            acc_ref[hh] = alpha * acc_ref[hh] + pv

    def consume_masked(src, b, diag_first):
        if diag_first:
            consume(src, b, mask=0, rows=t)
        else:
            consume(src, b, mask=1)

    @pl.when(i == 0)
    def _():
        key = lax.broadcasted_iota(jnp.int32, (kb, 2 * t), 0)
        qry = lax.broadcasted_iota(jnp.int32, (kb, 2 * t), 1)
        qry = jnp.where(qry >= t, qry - t, qry)
        zero = jnp.zeros((kb, 2 * t), F32)
        mask_ref[0] = jnp.where(key <= qry, zero, -jnp.inf)
        mask_ref[1] = jnp.where(key - t <= qry, zero, -jnp.inf)
        for hh in range(hps):
            row = cfeat_ref[pl.ds(hg * hps + hh, 1), :]
            q2_ref[hh, :, LANES:2 * LANES] = jnp.broadcast_to(row, (2 * t, LANES)).astype(BF16)
        build_q2(0)
        raw_scores(sx, 0)

    for hh in range(hps):
        m_ref[hh] = jnp.full((1, 2 * t), -jnp.inf, F32)
        acc_ref[hh] = jnp.zeros((VT_ROWS, 2 * t), F32)

    def body(pair, carry):
        b = 2 * pair
        raw_scores(sy, b + 1)
        consume(sx, b)
        raw_scores(sx, b + 2)
        consume(sy, b + 1)
        return carry

    lax.fori_loop(0, nb // 2, body, 0)

    def finish():
        build_q2(jnp.minimum(i + 1, last_tile))
        raw_scores(sx, 0)
        lam = lam_ref[0:1, 0:1]
        for hh in range(hps):
            on = acc_ref[hh, 0:vd, :] * (1.0 / acc_ref[hh, vd:vd + 1, :])
            o = jnp.transpose(on[:, 0:t] - lam * on[:, t:2 * t])
            o = ((o * _rms_scale(o)) * g_ref[...]) * out_scale
            o_ref[:, hh * vd:(hh + 1) * vd] = o.astype(o_ref.dtype)

    for diag_first in (True, False):
        @pl.when((nb % 2 == 1) & (parity == (0 if diag_first else 1)))
        def _():
            raw_scores(sy, nb)
            consume(sx, nb - 1)
            consume_masked(sy, nb, diag_first)

        @pl.when((nb % 2 == 0) & (parity == (0 if diag_first else 1)))
        def _():
            consume_masked(sx, nb, diag_first)
            finish()

    @pl.when(nb % 2 == 1)
    def _():
        finish()


def _diff_attention(q, k, vt, lam, subln_g, *, batch, seq, out_scale):
    n, width = q.shape
    heads = width // DIFF_V_DIM
    hps = HEADS_PER_STEP
    t = Q_TILE
    assert seq % KEY_BLOCK == 0 and heads % hps == 0 and DIFF_V_DIM == LANES
    nq = seq // t
    kfeat, cfeat = _alibi_features(seq, heads)
    scores = pltpu.VMEM((hps, KEY_BLOCK, 2 * t + LANES), F32)
    per_head_seq = pl.BlockSpec((seq, hps * DIFF_V_DIM), lambda b, h, i: (b, h))
    return pl.pallas_call(
        functools.partial(_attn_kernel, out_scale=out_scale),
        name="diff_attn",
        grid=(batch, heads // hps, nq),
        in_specs=[
            per_head_seq,
            per_head_seq,
            _resident(kfeat.shape),
            _resident(cfeat.shape),
            pl.BlockSpec((hps * VT_ROWS, seq), lambda b, h, i: (h, b)),
            _resident(lam.shape),
            _resident(subln_g.shape),
        ],
        out_specs=pl.BlockSpec((t, hps * DIFF_V_DIM), lambda b, h, i: (b * nq + i, h)),
        out_shape=jax.ShapeDtypeStruct((n, width), BF16),
        scratch_shapes=[
            pltpu.VMEM((hps, 2 * t, 2 * LANES), BF16),
            pltpu.VMEM((2, KEY_BLOCK, 2 * t), F32),
            scores,
            scores,
            pltpu.VMEM((hps, 1, 2 * t), F32),
            pltpu.VMEM((hps, 1, 2 * t), F32),
            pltpu.VMEM((hps, 1, 2 * t), F32),
            pltpu.VMEM((hps, VT_ROWS, 2 * t), F32),
        ],
        compiler_params=pltpu.CompilerParams(
            dimension_semantics=("arbitrary", "arbitrary", "arbitrary"),
            vmem_limit_bytes=VMEM_LIMIT_BYTES),
    )(q, k, kfeat, cfeat, vt, lam, subln_g)


def _out_mlp_kernel(x_ref, gaya_ref, gb_ref, on_ref, w_b_ref, w_o_ref, g_mlp_ref,
                    w1_ref, w2_ref, g_fin_ref, out_ref, w_b_bf_ref, w_o_bf_ref, *, final_norm):
    @pl.when(pl.program_id(0) == 0)
    def _():
        w_b_bf_ref[...] = w_b_ref[...].astype(BF16)
        w_o_bf_ref[...] = w_o_ref[...].astype(BF16)

    y_b = _dot(on_ref[...], w_b_bf_ref[...])
    merged = gaya_ref[...].astype(F32) + gb_ref[...].astype(F32) * y_b
    x1 = x_ref[...] + _dot(merged.astype(BF16), w_o_bf_ref[...])
    xn = ((x1 * _rms_scale(x1)) * g_mlp_ref[...]).astype(BF16)
    d_ff = w1_ref.shape[1]
    acc = x1
    for c in range(0, d_ff, FF_CHUNK):
        hc = jnp.maximum(_dot(xn, w1_ref[:, c:c + FF_CHUNK]), 0.0)
        acc = acc + _dot((hc * hc).astype(BF16), w2_ref[c:c + FF_CHUNK, :])
    if final_norm:
        acc = (acc * _rms_scale(acc)) * g_fin_ref[...]
    out_ref[...] = acc


def _out_mlp(x2d, gaya, gb, on, w_b, w_o, g_mlp, w1, w2, g_fin, *, final_norm):
    n, d_model = x2d.shape
    tm = TOKEN_TILE
    assert n % tm == 0 and w1.shape[1] % FF_CHUNK == 0
    row = lambda i: (i, 0)
    return pl.pallas_call(
        functools.partial(_out_mlp_kernel, final_norm=final_norm),
        name="out_mlp",
        grid=(n // tm,),
        in_specs=[
            pl.BlockSpec((tm, d_model), row),
            pl.BlockSpec((tm, d_model), row),
            pl.BlockSpec((tm, d_model), row),
            pl.BlockSpec((tm, on.shape[1]), row),
            _resident(w_b.shape),
            _resident(w_o.shape),
            _resident(g_mlp.shape),
            _resident(w1.shape),
            _resident(w2.shape),
            _resident(g_fin.shape),
        ],
        out_specs=pl.BlockSpec((tm, d_model), row),
        out_shape=jax.ShapeDtypeStruct((n, d_model), F32),
        scratch_shapes=[
            pltpu.VMEM(w_b.shape, BF16),
            pltpu.VMEM(w_o.shape, BF16),
        ],
        compiler_params=pltpu.CompilerParams(
            dimension_semantics=("arbitrary",), vmem_limit_bytes=VMEM_LIMIT_BYTES),
    )(x2d, gaya, gb, on, w_b, w_o, g_mlp, w1, w2, g_fin)


def _lambda_init(layer_idx):
    return 0.8 - 0.6 * math.exp(-0.3 * layer_idx)


def kernel(x, norm_mix_g, w_in, b_gate, conv_w, lambda_q1, lambda_k1, lambda_q2, lambda_k2,
           subln_g, w_a_out, w_b_out, w_o, norm_mlp_g, w_mlp_in, w_mlp_out, norm_final_g):
    batch, seq, d_model = x.shape
    depth = w_in.shape[0]
    x2d = x.reshape(batch * seq, d_model)
    row = lambda a: a.reshape(1, -1)
    for l in range(depth):
        lam_init = _lambda_init(l)
        q, k, vt, gaya, gb, lam = _in_proj(
            x2d, row(norm_mix_g[l]), w_in[l], row(b_gate[l]), conv_w[l],
            w_a_out[l], row(lambda_q1[l]), row(lambda_k1[l]),
            row(lambda_q2[l]), row(lambda_k2[l]), seq=seq, lam_init=lam_init)
        on = _diff_attention(q, k, vt, lam, row(subln_g[l]), batch=batch, seq=seq,
                             out_scale=1.0 - lam_init)
        x2d = _out_mlp(
            x2d, gaya, gb, on, w_b_out[l], w_o[l],
            row(norm_mlp_g[l]), w_mlp_in[l].astype(BF16), w_mlp_out[l].astype(BF16),
            row(norm_final_g), final_norm=(l == depth - 1))
    return x2d.reshape(batch, seq, d_model)
```
